```python
import math
import jax, jax.numpy as jnp
from jax import lax
import numpy as np

D_MODEL = 2048
BATCH = 8
SEQ = 4096
DEPTH = 2

MIX_WIDTH = D_MODEL
N_GROUPS_MIX = 4
GW = MIX_WIDTH // N_GROUPS_MIX
SSM_CH = 16
SSM_GROUPS = GW // SSM_CH
SSM_STATE = 64
DT_MIN = 1e-3
DT_MAX = 1e-1
CONV_WIDTH = 31
ATT_HEADS = 4
ATT_HEAD_DIM = GW // ATT_HEADS
IDX_HEADS = 8
IDX_DIM = 64
TOPK_MAX = 256
Q_BLOCK = 128
REL_BUCKETS = 32
REL_MAX_DIST = 128
MEM_TOKENS = 256
MEM_HEADS = 4
MEM_HEAD_DIM = GW // MEM_HEADS
DN_ALPHA = (2 * DEPTH) ** 0.25
DN_BETA = (8 * DEPTH) ** -0.25
LN_EPS = 1e-5
NEG_BIG = -1e30

SPLIT_SIZES = (
    GW, GW,
    GW, GW, GW,
    GW, GW, GW, GW,
    IDX_HEADS * IDX_DIM, IDX_DIM, IDX_HEADS,
    GW, GW,
)
IN_WIDTH = sum(SPLIT_SIZES)
SPLIT_OFFSETS = tuple(int(o) for o in np.cumsum(SPLIT_SIZES)[:-1])

kernel_name = "hybrid_s5_conformer_dsa_memory_deepnorm"


def layer_norm(x, g, b):
    xf = x.astype(jnp.float32)
    mu = jnp.mean(xf, axis=-1, keepdims=True)
    var = jnp.mean(jnp.square(xf - mu), axis=-1, keepdims=True)
    y = (xf - mu) * lax.rsqrt(var + LN_EPS)
    return (y * g.astype(jnp.float32) + b.astype(jnp.float32)).astype(x.dtype)


def _complex_affine_combine(e1, e2):
    ar1, ai1, br1, bi1 = e1
    ar2, ai2, br2, bi2 = e2
    return (ar2 * ar1 - ai2 * ai1,
            ar2 * ai1 + ai2 * ar1,
            ar2 * br1 - ai2 * bi1 + br2,
            ar2 * bi1 + ai2 * br1 + bi2)


def ssm_branch(u, lam_re, lam_im, log_dt, b_re, b_im, c_re, c_im, d_skip, w_glu):
    bsz, seq, _ = u.shape
    f32 = jnp.float32
    uf = u.astype(f32).reshape(bsz, seq, SSM_GROUPS, SSM_CH)
    lr = lam_re.astype(f32)
    li = lam_im.astype(f32)
    dt = jnp.exp(log_dt.astype(f32))[:, None]
    mag = jnp.exp(lr * dt)
    abar_re = mag * jnp.cos(li * dt)
    abar_im = mag * jnp.sin(li * dt)
    den = lr * lr + li * li
    nr = abar_re - 1.0
    f_re = (nr * lr + abar_im * li) / den
    f_im = (abar_im * lr - nr * li) / den
    br = b_re.astype(f32)
    bi = b_im.astype(f32)
    bbar_re = f_re[..., None] * br - f_im[..., None] * bi
    bbar_im = f_re[..., None] * bi + f_im[..., None] * br
    bu_re = jnp.einsum('bsgc,gnc->sbgn', uf, bbar_re)
    bu_im = jnp.einsum('bsgc,gnc->sbgn', uf, bbar_im)
    a_re = jnp.broadcast_to(abar_re[None, None], (seq, 1, SSM_GROUPS, SSM_STATE))
    a_im = jnp.broadcast_to(abar_im[None, None], (seq, 1, SSM_GROUPS, SSM_STATE))
    _, _, s_re, s_im = lax.associative_scan(
        _complex_affine_combine, (a_re, a_im, bu_re, bu_im), axis=0)
    y = (jnp.einsum('sbgn,gcn->bsgc', s_re, c_re.astype(f32))
         - jnp.einsum('sbgn,gcn->bsgc', s_im, c_im.astype(f32))
         + d_skip.astype(f32).reshape(SSM_GROUPS, SSM_CH) * uf)
    y = y.reshape(bsz, seq, GW).astype(u.dtype)
    a, g = jnp.split(y @ w_glu, 2, axis=-1)
    return a * jax.nn.sigmoid(g)


def conv_branch(val, glu_gate, conv_w, conv_b, ln_g, ln_b, w_pw):
    h = val * jax.nn.sigmoid(glu_gate)
    h = lax.conv_general_dilated(
        h, conv_w.astype(h.dtype), window_strides=(1,),
        padding=[(CONV_WIDTH - 1, 0)],
        dimension_numbers=('NWC', 'WIO', 'NWC'),
        feature_group_count=GW) + conv_b
    h = layer_norm(h, ln_g, ln_b)
    h = jax.nn.silu(h)
    return h @ w_pw


def rel_bucket(dist):
    n = jnp.maximum(dist, 0)
    max_exact = REL_BUCKETS // 2
    nf = jnp.maximum(n, 1).astype(jnp.float32)
    large = max_exact + (jnp.log(nf / max_exact) / math.log(REL_MAX_DIST / max_exact)
                         * (REL_BUCKETS - max_exact)).astype(jnp.int32)
    large = jnp.minimum(large, REL_BUCKETS - 1)
    return jnp.where(n < max_exact, n, large)


def dsa_branch(q, k, v, iq, ik, iw, rel_bias):
    bsz, seq, _ = q.shape
    f32 = jnp.float32
    n_keys = seq
    top_k = min(TOPK_MAX, n_keys // 4)
    nb = seq // Q_BLOCK
    k4 = k.reshape(bsz, seq, ATT_HEADS, ATT_HEAD_DIM)
    v4 = v.reshape(bsz, seq, ATT_HEADS, ATT_HEAD_DIM)
    ikf = ik.astype(f32)
    qb = q.reshape(bsz, nb, Q_BLOCK, ATT_HEADS, ATT_HEAD_DIM).transpose(1, 0, 2, 3, 4)
    iqb = iq.reshape(bsz, nb, Q_BLOCK, IDX_HEADS, IDX_DIM).transpose(1, 0, 2, 3, 4)
    iwb = iw.reshape(bsz, nb, Q_BLOCK, IDX_HEADS).transpose(1, 0, 2, 3)
    s_pos = jnp.arange(seq, dtype=jnp.int32)
    scale = ATT_HEAD_DIM ** -0.5
    gather = jax.vmap(lambda arr, idx: arr[idx])

    def block_fn(args):
        blk, q_blk, iq_blk, iw_blk = args
        t = blk * Q_BLOCK + jnp.arange(Q_BLOCK, dtype=jnp.int32)
        dots = jnp.einsum('bqhd,bsd->bqhs', iq_blk.astype(f32), ikf)
        score = jnp.einsum('bqh,bqhs->bqs', iw_blk.astype(f32), jax.nn.relu(dots))
        causal = s_pos[None, :] <= t[:, None]
        score = jnp.where(causal[None], score, -jnp.inf)
        _, sel = lax.top_k(score, top_k)
        k_sel = gather(k4, sel)
        v_sel = gather(v4, sel)
        logits = jnp.einsum('bqhd,bqkhd->bhqk', q_blk, k_sel).astype(f32) * scale
        dist = t[None, :, None] - sel
        bias = rel_bias.astype(f32)[rel_bucket(dist)]
        logits = logits + bias.transpose(0, 3, 1, 2)
        valid = (sel <= t[None, :, None])[:, None]
        logits = jnp.where(valid, logits, NEG_BIG)
        p = jax.nn.softmax(logits, axis=-1).astype(v.dtype)
        return jnp.einsum('bhqk,bqkhd->bqhd', p, v_sel)

    out = lax.map(block_fn, (jnp.arange(nb, dtype=jnp.int32), qb, iqb, iwb))
    return out.transpose(1, 0, 2, 3, 4).reshape(bsz, seq, GW)


def memory_branch(mq, mem, w_mem_kv):
    bsz, seq, _ = mq.shape
    mk, mv = jnp.split(mem @ w_mem_kv, 2, axis=-1)
    q4 = mq.reshape(bsz, seq, MEM_HEADS, MEM_HEAD_DIM)
    k4 = mk.reshape(bsz, -1, MEM_HEADS, MEM_HEAD_DIM)
    v4 = mv.reshape(bsz, -1, MEM_HEADS, MEM_HEAD_DIM)
    logits = jnp.einsum('bshd,bmhd->bhsm', q4, k4).astype(jnp.float32) * MEM_HEAD_DIM ** -0.5
    p = jax.nn.softmax(logits, axis=-1).astype(mv.dtype)
    return jnp.einsum('bhsm,bmhd->bshd', p, v4).reshape(bsz, seq, GW)


def setup_inputs(seed: int = 0) -> dict:
    key = jax.random.key(seed)
    ks = jax.random.split(key, 24)
    f32 = jnp.float32
    nrm = lambda k, shape, s: jax.random.normal(k, shape, f32) * s
    n_idx = jnp.arange(SSM_STATE, dtype=f32)
    lam_re = -0.5 + nrm(ks[2], (DEPTH, SSM_GROUPS, SSM_STATE), 0.01)
    lam_im = math.pi * n_idx[None, None, :] + nrm(ks[3], (DEPTH, SSM_GROUPS, SSM_STATE), 0.01)
    log_dt = jax.random.uniform(ks[4], (DEPTH, SSM_GROUPS), f32,
                                math.log(DT_MIN), math.log(DT_MAX))
    return {
        "x": nrm(ks[0], (BATCH, SEQ, D_MODEL), 1.0),
        "mem": nrm(ks[1], (BATCH, MEM_TOKENS, D_MODEL), 1.0),
        "w_in": nrm(ks[5], (DEPTH, D_MODEL, IN_WIDTH), D_MODEL ** -0.5),
        "ssm_lam_re": lam_re,
        "ssm_lam_im": lam_im,
        "ssm_log_dt": log_dt,
        "ssm_b_re": nrm(ks[6], (DEPTH, SSM_GROUPS, SSM_STATE, SSM_CH), (2 * SSM_CH) ** -0.5),
        "ssm_b_im": nrm(ks[7], (DEPTH, SSM_GROUPS, SSM_STATE, SSM_CH), (2 * SSM_CH) ** -0.5),
        "ssm_c_re": nrm(ks[8], (DEPTH, SSM_GROUPS, SSM_CH, SSM_STATE), SSM_STATE ** -0.5),
        "ssm_c_im": nrm(ks[9], (DEPTH, SSM_GROUPS, SSM_CH, SSM_STATE), SSM_STATE ** -0.5),
        "ssm_d": nrm(ks[10], (DEPTH, GW), 1.0),
        "ssm_w_glu": nrm(ks[11], (DEPTH, GW, 2 * GW), GW ** -0.5),
        "conv_w": nrm(ks[12], (DEPTH, CONV_WIDTH, 1, GW), CONV_WIDTH ** -0.5),
        "conv_b": nrm(ks[13], (DEPTH, GW), 0.01),
        "conv_ln_g": 1.0 + nrm(ks[14], (DEPTH, GW), 0.01),
        "conv_ln_b": nrm(ks[15], (DEPTH, GW), 0.01),
        "conv_w_pw": nrm(ks[16], (DEPTH, GW, GW), GW ** -0.5),
        "rel_bias": nrm(ks[17], (REL_BUCKETS, ATT_HEADS), 0.2),
        "mem_w_kv": nrm(ks[18], (DEPTH, D_MODEL, 2 * GW), D_MODEL ** -0.5),
        "w_out": nrm(ks[19], (DEPTH, N_GROUPS_MIX * GW, D_MODEL), DN_BETA * (N_GROUPS_MIX * GW) ** -0.5),
        "ln_g": 1.0 + nrm(ks[20], (DEPTH, D_MODEL), 0.01),
        "ln_b": nrm(ks[21], (DEPTH, D_MODEL), 0.01),
    }


def reference(x, mem, w_in, ssm_lam_re, ssm_lam_im, ssm_log_dt, ssm_b_re, ssm_b_im,
              ssm_c_re, ssm_c_im, ssm_d, ssm_w_glu, conv_w, conv_b, conv_ln_g, conv_ln_b,
              conv_w_pw, rel_bias, mem_w_kv, w_out, ln_g, ln_b):
    for l in range(DEPTH):
        proj = x @ w_in[l]
        (u_a, g_a, val_b, glu_b, g_b, q_c, k_c, v_c, g_c,
         iq_c, ik_c, iw_c, q_d, g_d) = jnp.split(proj, SPLIT_OFFSETS, axis=-1)
        y_a = ssm_branch(u_a, ssm_lam_re[l], ssm_lam_im[l], ssm_log_dt[l], ssm_b_re[l],
                         ssm_b_im[l], ssm_c_re[l], ssm_c_im[l], ssm_d[l], ssm_w_glu[l])
        y_b = conv_branch(val_b, glu_b, conv_w[l], conv_b[l], conv_ln_g[l], conv_ln_b[l],
                          conv_w_pw[l])
        y_c = dsa_branch(q_c, k_c, v_c, iq_c, ik_c, iw_c, rel_bias)
        y_d = memory_branch(q_d, mem, mem_w_kv[l])
        mixed = jnp.concatenate([y_a * jax.nn.silu(g_a), y_b * jax.nn.silu(g_b),
                                 y_c * jax.nn.silu(g_c), y_d * jax.nn.silu(g_d)], axis=-1)
        x = layer_norm(DN_ALPHA * x + mixed @ w_out[l], ln_g[l], ln_b[l])
    return x
```

```python
import functools
import math

import jax
import jax.numpy as jnp
import numpy as np
from jax import lax
from jax.experimental import pallas as pl
from jax.experimental.pallas import tpu as pltpu

F32 = jnp.float32
BF16 = jnp.bfloat16
I32 = jnp.int32

D_MODEL = 2048
GW = 512
SSM_CH = 16
SSM_GROUPS = GW // SSM_CH
SSM_STATE = 64
SSM_COMPLEX = SSM_GROUPS * SSM_STATE
DT_MIN = 1e-3
CONV_WIDTH = 31
CONV_HALO = 32
ATT_HEADS = 4
HEAD_DIM = GW // ATT_HEADS
IDX_HEADS = 8
IDX_DIM = 64
TOPK_MAX = 256
REL_BUCKETS = 32
REL_MAX_DIST = 128
MEM_HEADS = 4
LN_EPS = 1e-5
NEG_BIG = -1e30

LANES = 128
COL_U, COL_GA, COL_VAL, COL_GLU, COL_GB, COL_Q, COL_K, COL_V, COL_GC, COL_IQ, COL_MQ, COL_GD = range(12)
COL_IKA = 12 * (GW // LANES)
COL_IKB = COL_IKA + 1
COL_IW = COL_IKA + 2
PACKED_WIDTH = 13 * GW

SIGN_BIT = -2147483648
KEY_NEG_INF = int(np.array(0xFF800000, np.uint32).astype(np.int64) ^ 0x7FFFFFFF) - (1 << 32)

VMEM_LIMIT = 56 * 1024 * 1024


def _sigmoid(x):
    return 1.0 / (1.0 + jnp.exp(-x))


def _silu(x):
    return x * _sigmoid(x)


def _matmul_kernel(x_ref, w_ref, o_ref, xb_ref):
    @pl.when(pl.program_id(1) == 0)
    def _():
        xb_ref[...] = x_ref[...].astype(BF16)

    o_ref[...] = jnp.dot(xb_ref[...], w_ref[...], preferred_element_type=F32).astype(o_ref.dtype)


def _matmul(x, w, tm, tn, name):
    m, k = x.shape
    n = w.shape[1]
    return pl.pallas_call(
        _matmul_kernel,
        grid=(m // tm, n // tn),
        in_specs=[pl.BlockSpec((tm, k), lambda i, j: (i, 0)),
                  pl.BlockSpec((k, tn), lambda i, j: (0, j))],
        out_specs=pl.BlockSpec((tm, tn), lambda i, j: (i, j)),
        out_shape=jax.ShapeDtypeStruct((m, n), BF16),
        scratch_shapes=[pltpu.VMEM((tm, k), BF16)],
        compiler_params=pltpu.CompilerParams(
            dimension_semantics=("parallel", "arbitrary"), vmem_limit_bytes=VMEM_LIMIT),
        name=name,
    )(x, w)


def _ssm_kernel(u_ref, g_ref, bt_ref, cre_ref, cim_ref, abar_ref, d_ref, wglu_ref, o_ref,
                st_ref, bu_ref, *, nb, lc):
    rows = nb * lc
    n_re = SSM_COMPLEX

    @pl.when(pl.program_id(0) == 0)
    def _():
        st_ref[...] = jnp.zeros_like(st_ref)

    n_ct = n_re // LANES
    u = u_ref[...].reshape(rows, GW)
    for jt in range(2 * n_re // 256):
        k0 = LANES * ((jt % (n_re // 256)) // 2)
        res = jnp.dot(u[:, k0:k0 + LANES], bt_ref[jt], preferred_element_type=F32)
        bu_ref[2 * jt] = res[:, :LANES]
        bu_ref[2 * jt + 1] = res[:, LANES:]

    slab = 4
    for ch in range(n_ct // slab):
        tiles = tuple(range(slab * ch, slab * (ch + 1)))
        a_re = [jnp.broadcast_to(abar_ref[:, LANES * c:LANES * (c + 1)], (nb, LANES)) for c in tiles]
        a_im = [jnp.broadcast_to(abar_ref[:, n_re + LANES * c:n_re + LANES * (c + 1)], (nb, LANES))
                for c in tiles]

        def step(t, carry, tiles=tiles, a_re=a_re, a_im=a_im):
            r = pl.ds(t, nb, stride=lc)
            new = []
            for n, c in enumerate(tiles):
                h_re, h_im = carry[n]
                n_hre = a_re[n] * h_re - a_im[n] * h_im + bu_ref[c, r, :]
                n_him = a_re[n] * h_im + a_im[n] * h_re + bu_ref[n_ct + c, r, :]
                bu_ref[c, r, :] = n_hre
                bu_ref[n_ct + c, r, :] = n_him
                new.append((n_hre, n_him))
            return tuple(new)

        init = tuple((st_ref[c], st_ref[n_ct + c]) for c in tiles)
        final = lax.fori_loop(0, lc, step, init)
        for n, c in enumerate(tiles):
            st_ref[c] = final[n][0]
            st_ref[n_ct + c] = final[n][1]

    ys = []
    for m in range(GW // LANES):
        s_re = jnp.concatenate([bu_ref[4 * m + n] for n in range(4)], axis=1).astype(BF16)
        s_im = jnp.concatenate([bu_ref[n_ct + 4 * m + n] for n in range(4)], axis=1).astype(BF16)
        ys.append(jnp.dot(s_re, cre_ref[m], preferred_element_type=F32)
                  + jnp.dot(s_im, cim_ref[m], preferred_element_type=F32))
    y = jnp.concatenate(ys, axis=1) + d_ref[...] * u.astype(F32)
    z = jnp.dot(y.astype(BF16), wglu_ref[...], preferred_element_type=F32)
    gate = g_ref[...].reshape(rows, GW).astype(F32)
    out = z[:, :GW] * _sigmoid(z[:, GW:]) * _silu(gate)
    o_ref[...] = out.reshape(nb, lc, GW).astype(o_ref.dtype)


def _ssm_params(lam_re, lam_im, log_dt, b_re, b_im, c_re, c_im):
    lr = lam_re.astype(F32)
    li = lam_im.astype(F32)
    dt = jnp.exp(log_dt.astype(F32))[:, None]
    mag = jnp.exp(lr * dt)
    abar_re = mag * jnp.cos(li * dt)
    abar_im = mag * jnp.sin(li * dt)
    den = lr * lr + li * li
    nr = abar_re - 1.0
    f_re = (nr * lr + abar_im * li) / den
    f_im = (abar_im * lr - nr * li) / den
    br = b_re.astype(F32)
    bi = b_im.astype(F32)
    bbar_re = f_re[..., None] * br - f_im[..., None] * bi
    bbar_im = f_re[..., None] * bi + f_im[..., None] * br
    eye = jnp.eye(SSM_GROUPS, dtype=F32)
    bfull = jnp.concatenate(
        [jnp.einsum('gnc,gh->gchn', b, eye).reshape(GW, SSM_COMPLEX) for b in (bbar_re, bbar_im)], axis=1)
    n_tiles = SSM_COMPLEX // 256
    bt = jnp.stack([
        bfull[LANES * ((jt % n_tiles) // 2):LANES * ((jt % n_tiles) // 2 + 1), 256 * jt:256 * (jt + 1)]
        for jt in range(2 * n_tiles)]).astype(BF16)
    cfull_re = jnp.einsum('gcn,gh->gnhc', c_re.astype(F32), eye).reshape(SSM_COMPLEX, GW)
    cfull_im = -jnp.einsum('gcn,gh->gnhc', c_im.astype(F32), eye).reshape(SSM_COMPLEX, GW)
    cre = jnp.stack([cfull_re[512 * m:512 * (m + 1), LANES * m:LANES * (m + 1)]
                     for m in range(GW // LANES)]).astype(BF16)
    cim = jnp.stack([cfull_im[512 * m:512 * (m + 1), LANES * m:LANES * (m + 1)]
                     for m in range(GW // LANES)]).astype(BF16)
    abar = jnp.concatenate([abar_re.reshape(1, SSM_COMPLEX), abar_im.reshape(1, SSM_COMPLEX)], axis=1)
    return bt, cre, cim, abar


def _ssm_call(proj3, bt, cre, cim, abar, d_skip, w_glu, lc):
    nb, seq, _ = proj3.shape
    kern = functools.partial(_ssm_kernel, nb=nb, lc=lc)
    const = lambda *shape: pl.BlockSpec(shape, lambda c: (0,) * len(shape))
    return pl.pallas_call(
        kern,
        grid=(seq // lc,),
        in_specs=[pl.BlockSpec((nb, lc, GW), lambda c: (0, c, COL_U)),
                  pl.BlockSpec((nb, lc, GW), lambda c: (0, c, COL_GA)),
                  const(*bt.shape), const(*cre.shape), const(*cim.shape), const(*abar.shape),
                  const(1, GW), const(GW, 2 * GW)],
        out_specs=pl.BlockSpec((nb, lc, GW), lambda c: (0, c, 0)),
        out_shape=jax.ShapeDtypeStruct((nb, seq, GW), BF16),
        scratch_shapes=[pltpu.VMEM((2 * SSM_COMPLEX // LANES, nb, LANES), F32),
                        pltpu.VMEM((2 * SSM_COMPLEX // LANES, nb * lc, LANES), F32)],
        compiler_params=pltpu.CompilerParams(
            dimension_semantics=("arbitrary",), vmem_limit_bytes=VMEM_LIMIT),
        name="ssm_group",
    )(proj3, proj3, bt, cre, cim, abar, d_skip, w_glu)


def _conv_kernel(val_ref, glu_ref, g_ref, cw_ref, cb_ref, lg_ref, lb_ref, wpw_ref, o_ref, h_ref, *, tc):
    @pl.when(pl.program_id(1) == 0)
    def _():
        h_ref[0:CONV_HALO, :] = jnp.zeros((CONV_HALO, GW), F32)

    @pl.when(pl.program_id(1) != 0)
    def _():
        h_ref[0:CONV_HALO, :] = h_ref[tc:tc + CONV_HALO, :]

    val = val_ref[0].astype(F32)
    h_ref[CONV_HALO:CONV_HALO + tc, :] = val * _sigmoid(glu_ref[0].astype(F32))

    rc = 64
    first = CONV_HALO - (CONV_WIDTH - 1)
    for r0 in range(0, tc, rc):
        acc = jnp.zeros((rc, GW), F32)
        for k in range(CONV_WIDTH):
            acc = acc + cw_ref[k:k + 1, :] * h_ref[r0 + first + k:r0 + first + k + rc, :]
        hc = acc + cb_ref[...]
        mu = jnp.mean(hc, axis=-1, keepdims=True)
        xc = hc - mu
        var = jnp.mean(xc * xc, axis=-1, keepdims=True)
        hn = xc * lax.rsqrt(var + LN_EPS) * lg_ref[...] + lb_ref[...]
        hs = _silu(hn)
        y = jnp.dot(hs.astype(BF16), wpw_ref[...], preferred_element_type=F32)
        o_ref[0, r0:r0 + rc, :] = (y * _silu(g_ref[0, r0:r0 + rc, :].astype(F32))).astype(o_ref.dtype)


def _conv_call(proj3, cw, cb, lg, lb, wpw, tc):
    nb, seq, _ = proj3.shape
    kern = functools.partial(_conv_kernel, tc=tc)
    const = lambda *shape: pl.BlockSpec(shape, lambda b, s: (0,) * len(shape))
    return pl.pallas_call(
        kern,
        grid=(nb, seq // tc),
        in_specs=[pl.BlockSpec((1, tc, GW), lambda b, s: (b, s, COL_VAL)),
                  pl.BlockSpec((1, tc, GW), lambda b, s: (b, s, COL_GLU)),
                  pl.BlockSpec((1, tc, GW), lambda b, s: (b, s, COL_GB)),
                  const(CONV_HALO, GW), const(1, GW), const(1, GW), const(1, GW), const(GW, GW)],
        out_specs=pl.BlockSpec((1, tc, GW), lambda b, s: (b, s, 0)),
        out_shape=jax.ShapeDtypeStruct((nb, seq, GW), BF16),
        scratch_shapes=[pltpu.VMEM((tc + CONV_HALO, GW), F32)],
        compiler_params=pltpu.CompilerParams(
            dimension_semantics=("parallel", "arbitrary"), vmem_limit_bytes=VMEM_LIMIT),
        name="conv_group",
    )(proj3, proj3, proj3, cw, cb, lg, lb, wpw)


def _dsa_kernel(q_ref, g_ref, iq_ref, iw_ref, k_ref, v_ref, ika_ref, ikb_ref, bias_ref, o_ref,
                key_scr, iwb_scr, tau_scr, cand_scr, cnt_scr, tauw_scr, m_scr, l_scr, acc_scr,
                *, tb, topk, idx_bits):
    i = pl.program_id(1)
    nkb = i + 1
    halves = tb // LANES
    nt = (((1,), (1,)), ((), ()))
    scale = HEAD_DIM ** -0.5

    iw = iw_ref[0].astype(F32)
    for h in range(IDX_HEADS):
        iwb_scr[h] = jnp.broadcast_to(iw[:, h:h + 1], (tb, tb))
    t_idx = i * tb + lax.broadcasted_iota(I32, (tb, tb), 0)
    lane_idx = lax.broadcasted_iota(I32, (tb, tb), 1)

    def score_body(j, carry):
        off = pl.multiple_of(j * tb, tb)
        ka = ika_ref[0, pl.ds(off, tb), :]
        kb = ikb_ref[0, pl.ds(off, tb), :]
        sc = jnp.zeros((tb, tb), F32)
        for p in range(IDX_HEADS // 2):
            iqp = iq_ref[0, :, LANES * p:LANES * (p + 1)]
            da = lax.dot_general(iqp, ka, nt, preferred_element_type=F32)
            db = lax.dot_general(iqp, kb, nt, preferred_element_type=F32)
            sc = sc + iwb_scr[2 * p] * jnp.maximum(da, 0.0) + iwb_scr[2 * p + 1] * jnp.maximum(db, 0.0)
        sc = jnp.where(off + lane_idx <= t_idx, sc, -jnp.inf)
        bits = pltpu.bitcast(sc, I32)
        key = bits ^ ((bits >> 31) & 0x7FFFFFFF)
        key_scr[j] = jnp.where(key == -1, 0, key)
        return carry

    lax.fori_loop(0, nkb, score_body, 0)

    ones = jnp.ones((LANES, LANES), BF16)

    def count(pred):
        cnt_scr[...] = jnp.zeros((tb, LANES), F32)

        def body(j, carry):
            kt = key_scr[j]
            c = cand_scr[...]
            part = cnt_scr[...]
            for hf in range(halves):
                part = part + jnp.where(pred(kt[:, LANES * hf:LANES * (hf + 1)], c, j * tb + LANES * hf), 1.0, 0.0)
            cnt_scr[...] = part
            return carry

        lax.fori_loop(0, nkb, body, 0)
        return jnp.dot(cnt_scr[...].astype(BF16), ones, preferred_element_type=F32)

    tau_scr[...] = jnp.zeros((tb, LANES), I32)

    def search_body(it, carry):
        bit = lax.shift_left(jnp.int32(1), 31 - it)
        cand_u = tau_scr[...] | bit
        cand_scr[...] = cand_u ^ SIGN_BIT
        cnt = count(lambda kt, c, base: kt >= c)
        tau_scr[...] = jnp.where(cnt >= topk, cand_u, tau_scr[...])
        return carry

    lax.fori_loop(0, 32, search_body, 0)
    tau = jnp.maximum(tau_scr[...] ^ SIGN_BIT, KEY_NEG_INF + 1)

    cand_scr[...] = tau
    need = topk - count(lambda kt, c, base: kt > c)
    n_eq = count(lambda kt, c, base: kt == c)
    any_excess = jnp.max(jnp.where(n_eq > need, 1.0, 0.0)) > 0.0

    @pl.when(any_excess)
    def _():
        lane128 = lax.broadcasted_iota(I32, (tb, LANES), 1)
        tauw_scr[:, 0:LANES] = jnp.zeros((tb, LANES), I32)

        def idx_body(it, carry):
            bit = lax.shift_left(jnp.int32(1), idx_bits - 1 - it)
            cand_m = tauw_scr[:, 0:LANES] | bit
            tauw_scr[:, LANES:2 * LANES] = cand_m
            cnt = count(lambda kt, c, base: (kt == c) & (base + lane128 < tauw_scr[:, LANES:2 * LANES]))
            tauw_scr[:, 0:LANES] = jnp.where(cnt < need, cand_m, tauw_scr[:, 0:LANES])
            return carry

        lax.fori_loop(0, idx_bits, idx_body, 0)
        last = tauw_scr[:, 0:LANES]

        def demote_body(j, carry):
            kt = key_scr[j]
            parts = []
            for hf in range(halves):
                kh = kt[:, LANES * hf:LANES * (hf + 1)]
                drop = (kh == tau) & (j * tb + LANES * hf + lane128 > last)
                parts.append(jnp.where(drop, kh - 1, kh))
            key_scr[j] = jnp.concatenate(parts, axis=1)
            return carry

        lax.fori_loop(0, nkb, demote_body, 0)

    tauw_scr[...] = jnp.concatenate([tau] * halves, axis=1)
    m_scr[...] = jnp.full(m_scr.shape, NEG_BIG, F32)
    l_scr[...] = jnp.zeros(l_scr.shape, F32)
    acc_scr[...] = jnp.zeros(acc_scr.shape, F32)

    def att_body(j, carry):
        off = pl.multiple_of(j * tb, tb)
        sel = key_scr[j] >= tauw_scr[...]
        bsel = jnp.minimum(i - j, 2)
        for h in range(ATT_HEADS):
            c0 = HEAD_DIM * h
            qh = q_ref[0, :, c0:c0 + HEAD_DIM]
            kh = k_ref[0, pl.ds(off, tb), c0:c0 + HEAD_DIM]
            vh = v_ref[0, pl.ds(off, tb), c0:c0 + HEAD_DIM]
            s = lax.dot_general(qh, kh, nt, preferred_element_type=F32) * scale + bias_ref[bsel, h]
            s = jnp.where(sel, s, NEG_BIG)
            m_prev = m_scr[h]
            m_new = jnp.maximum(m_prev, jnp.max(s, axis=1, keepdims=True))
            alpha = jnp.exp(m_prev - m_new)
            p = jnp.exp(s - jnp.concatenate([m_new] * halves, axis=1))
            l_scr[h] = alpha * l_scr[h] + jnp.sum(p, axis=1, keepdims=True)
            acc_scr[h] = alpha * acc_scr[h] + jnp.dot(p.astype(BF16), vh, preferred_element_type=F32)
            m_scr[h] = m_new
        return carry

    lax.fori_loop(0, nkb, att_body, 0)

    y = jnp.concatenate([acc_scr[h] / l_scr[h] for h in range(ATT_HEADS)], axis=1)
    o_ref[0] = (y * _silu(g_ref[0].astype(F32))).astype(o_ref.dtype)


def _rel_bucket(dist):
    n = jnp.maximum(dist, 0)
    max_exact = REL_BUCKETS // 2
    nf = jnp.maximum(n, 1).astype(F32)
    large = max_exact + (jnp.log(nf / max_exact) / math.log(REL_MAX_DIST / max_exact)
                         * (REL_BUCKETS - max_exact)).astype(I32)
    large = jnp.minimum(large, REL_BUCKETS - 1)
    return jnp.where(n < max_exact, n, large)


def _bias_tiles(rel_bias, tb):
    assert tb >= REL_MAX_DIST
    q = jnp.arange(tb, dtype=I32)[:, None]
    s = jnp.arange(tb, dtype=I32)[None, :]
    tiles = [rel_bias.astype(F32)[_rel_bucket(delta + q - s)].transpose(2, 0, 1)
             for delta in (0, tb, 2 * tb)]
    return jnp.stack(tiles)


def _dsa_call(proj3, bias, tb, topk):
    nb, seq, _ = proj3.shape
    nkb = seq // tb
    kern = functools.partial(_dsa_kernel, tb=tb, topk=float(topk), idx_bits=max(1, (seq - 1).bit_length()))
    qblk = lambda col: pl.BlockSpec((1, tb, GW), lambda b, i: (b, i, col))
    return pl.pallas_call(
        kern,
        grid=(nb, nkb),
        in_specs=[qblk(COL_Q), qblk(COL_GC), qblk(COL_IQ),
                  pl.BlockSpec((1, tb, LANES), lambda b, i: (b, i, COL_IW)),
                  pl.BlockSpec((1, seq, GW), lambda b, i: (b, 0, COL_K)),
                  pl.BlockSpec((1, seq, GW), lambda b, i: (b, 0, COL_V)),
                  pl.BlockSpec((1, seq, LANES), lambda b, i: (b, 0, COL_IKA)),
                  pl.BlockSpec((1, seq, LANES), lambda b, i: (b, 0, COL_IKB)),
                  pl.BlockSpec(bias.shape, lambda b, i: (0, 0, 0, 0))],
        out_specs=pl.BlockSpec((1, tb, GW), lambda b, i: (b, i, 0)),
        out_shape=jax.ShapeDtypeStruct((nb, seq, GW), BF16),
        scratch_shapes=[pltpu.VMEM((nkb, tb, tb), I32),
                        pltpu.VMEM((IDX_HEADS, tb, tb), F32),
                        pltpu.VMEM((tb, LANES), I32),
                        pltpu.VMEM((tb, LANES), I32),
                        pltpu.VMEM((tb, LANES), F32),
                        pltpu.VMEM((tb, tb), I32),
                        pltpu.VMEM((ATT_HEADS, tb, LANES), F32),
                        pltpu.VMEM((ATT_HEADS, tb, LANES), F32),
                        pltpu.VMEM((ATT_HEADS, tb, HEAD_DIM), F32)],
        compiler_params=pltpu.CompilerParams(
            dimension_semantics=("parallel", "arbitrary"), vmem_limit_bytes=VMEM_LIMIT),
        name="dsa_group",
    )(proj3, proj3, proj3, proj3, proj3, proj3, proj3, proj3, bias)


def _mem_kernel(q_ref, g_ref, kv_ref, o_ref):
    nt = (((1,), (1,)), ((), ()))
    scale = HEAD_DIM ** -0.5
    outs = []
    for h in range(MEM_HEADS):
        c0 = HEAD_DIM * h
        qh = q_ref[0, :, c0:c0 + HEAD_DIM]
        kh = kv_ref[0, :, c0:c0 + HEAD_DIM]
        vh = kv_ref[0, :, GW + c0:GW + c0 + HEAD_DIM]
        s = lax.dot_general(qh, kh, nt, preferred_element_type=F32) * scale
        m = jnp.max(s, axis=1, keepdims=True)
        p = jnp.exp(s - m)
        l = jnp.sum(p, axis=1, keepdims=True)
        outs.append(jnp.dot(p.astype(BF16), vh, preferred_element_type=F32) / l)
    y = jnp.concatenate(outs, axis=1)
    o_ref[0] = (y * _silu(g_ref[0].astype(F32))).astype(o_ref.dtype)


def _mem_call(proj3, mkv, tm):
    nb, seq, _ = proj3.shape
    n_mem = mkv.shape[1]
    return pl.pallas_call(
        _mem_kernel,
        grid=(nb, seq // tm),
        in_specs=[pl.BlockSpec((1, tm, GW), lambda b, s: (b, s, COL_MQ)),
                  pl.BlockSpec((1, tm, GW), lambda b, s: (b, s, COL_GD)),
                  pl.BlockSpec((1, n_mem, 2 * GW), lambda b, s: (b, 0, 0))],
        out_specs=pl.BlockSpec((1, tm, GW), lambda b, s: (b, s, 0)),
        out_shape=jax.ShapeDtypeStruct((nb, seq, GW), BF16),
        compiler_params=pltpu.CompilerParams(
            dimension_semantics=("parallel", "parallel"), vmem_limit_bytes=VMEM_LIMIT),
        name="mem_group",
    )(proj3, proj3, mkv)


def _out_kernel(ya_ref, yb_ref, yc_ref, yd_ref, x_ref, w_ref, g_ref, b_ref, o_ref, *, alpha):
    acc = alpha * x_ref[...]
    for n, y_ref in enumerate((ya_ref, yb_ref, yc_ref, yd_ref)):
        acc = acc + jnp.dot(y_ref[...], w_ref[GW * n:GW * (n + 1), :], preferred_element_type=F32)
    mu = jnp.mean(acc, axis=-1, keepdims=True)
    xc = acc - mu
    var = jnp.mean(xc * xc, axis=-1, keepdims=True)
    o_ref[...] = xc * lax.rsqrt(var + LN_EPS) * g_ref[...] + b_ref[...]


def _out_call(ys, x2, w_out, ln_g, ln_b, alpha, tm):
    t = x2.shape[0]
    kern = functools.partial(_out_kernel, alpha=alpha)
    row = lambda width: pl.BlockSpec((tm, width), lambda r: (r, 0))
    const = lambda *shape: pl.BlockSpec(shape, lambda r: (0,) * len(shape))
    return pl.pallas_call(
        kern,
        grid=(t // tm,),
        in_specs=[row(GW), row(GW), row(GW), row(GW), row(D_MODEL),
                  const(4 * GW, D_MODEL), const(1, D_MODEL), const(1, D_MODEL)],
        out_specs=row(D_MODEL),
        out_shape=jax.ShapeDtypeStruct((t, D_MODEL), F32),
        compiler_params=pltpu.CompilerParams(
            dimension_semantics=("parallel",), vmem_limit_bytes=VMEM_LIMIT),
        name="out_deepnorm",
    )(*ys, x2, w_out, ln_g, ln_b)


def _pack_w_in(w):
    d = w.shape[0]
    ik0 = 9 * GW + IDX_HEADS * IDX_DIM
    iw0 = ik0 + IDX_DIM
    md0 = iw0 + IDX_HEADS
    w_ik = w[:, ik0:iw0]
    zeros = lambda n: jnp.zeros((d, n), w.dtype)
    packed = jnp.concatenate(
        [w[:, :ik0], w[:, md0:md0 + 2 * GW],
         w_ik, zeros(LANES - IDX_DIM),
         zeros(LANES - IDX_DIM), w_ik,
         w[:, iw0:md0], zeros(LANES - IDX_HEADS),
         zeros(PACKED_WIDTH - 12 * GW - 3 * LANES)], axis=1)
    return packed.astype(BF16)


def _pick(n, pref):
    while n % pref:
        pref //= 2
    return pref


def kernel(x, mem, w_in, ssm_lam_re, ssm_lam_im, ssm_log_dt, ssm_b_re, ssm_b_im, ssm_c_re, ssm_c_im,
           ssm_d, ssm_w_glu, conv_w, conv_b, conv_ln_g, conv_ln_b, conv_w_pw, rel_bias, mem_w_kv,
           w_out, ln_g, ln_b):
    nb, seq, d = x.shape
    depth = w_in.shape[0]
    t = nb * seq
    alpha = (2 * depth) ** 0.25
    topk = min(TOPK_MAX, seq // 4)
    tb = 256
    tm = _pick(t, 1024)
    n_mem = mem.shape[1]

    bias = _bias_tiles(rel_bias, tb)
    mem2 = mem.reshape(nb * n_mem, d)
    x2 = x.reshape(t, d)
    for l in range(depth):
        proj = _matmul(x2, _pack_w_in(w_in[l]), tm, GW, "in_proj")
        proj3 = proj.reshape(nb, seq, PACKED_WIDTH)
        mkv = _matmul(mem2, mem_w_kv[l].astype(BF16), _pick(nb * n_mem, 1024), GW, "mem_kv")
        mkv = mkv.reshape(nb, n_mem, 2 * GW)

        bt, cre, cim, abar = _ssm_params(ssm_lam_re[l], ssm_lam_im[l], ssm_log_dt[l], ssm_b_re[l],
                                         ssm_b_im[l], ssm_c_re[l], ssm_c_im[l])
        y_a = _ssm_call(proj3, bt, cre, cim, abar, ssm_d[l].reshape(1, GW).astype(F32),
                        ssm_w_glu[l].astype(BF16), _pick(seq, 128))
        cw = jnp.concatenate([conv_w[l].reshape(CONV_WIDTH, GW).astype(F32),
                              jnp.zeros((CONV_HALO - CONV_WIDTH, GW), F32)], axis=0)
        y_b = _conv_call(proj3, cw, conv_b[l].reshape(1, GW).astype(F32),
                         conv_ln_g[l].reshape(1, GW).astype(F32), conv_ln_b[l].reshape(1, GW).astype(F32),
                         conv_w_pw[l].astype(BF16), _pick(seq, 512))
        y_c = _dsa_call(proj3, bias, tb, topk)
        y_d = _mem_call(proj3, mkv, _pick(seq, 512))

        ys = [y.reshape(t, GW) for y in (y_a, y_b, y_c, y_d)]
        x2 = _out_call(ys, x2, w_out[l].astype(BF16), ln_g[l].reshape(1, d).astype(F32),
                       ln_b[l].reshape(1, d).astype(F32), alpha, _pick(t, 512))
    return x2.reshape(nb, seq, d)
```

```python
import functools
import math

import jax
import jax.numpy as jnp
import numpy as np
from jax import lax
from jax.experimental import pallas as pl
from jax.experimental.pallas import tpu as pltpu

F32 = jnp.float32
BF16 = jnp.bfloat16
I32 = jnp.int32

D_MODEL = 2048
GW = 512
SSM_CH = 16
SSM_GROUPS = GW // SSM_CH
SSM_STATE = 64
SSM_COMPLEX = SSM_GROUPS * SSM_STATE
DT_MIN = 1e-3
CONV_WIDTH = 31
CONV_HALO = 32
ATT_HEADS = 4
HEAD_DIM = GW // ATT_HEADS
IDX_HEADS = 8
IDX_DIM = 64
TOPK_MAX = 256
REL_BUCKETS = 32
REL_MAX_DIST = 128
MEM_HEADS = 4
LN_EPS = 1e-5
NEG_BIG = -1e30
LOG2_E = math.log2(math.e)

LANES = 128
COL_U, COL_GA, COL_VAL, COL_GLU, COL_GB, COL_Q, COL_K, COL_V, COL_GC, COL_IQ, COL_MQ, COL_GD = range(12)
COL_IKA = 12 * (GW // LANES)
COL_IKB = COL_IKA + 1
COL_IW = COL_IKA + 2
PACKED_WIDTH = 13 * GW

SIGN_BIT = -2147483648
KEY_NEG_INF = int(np.array(0xFF800000, np.uint32).astype(np.int64) ^ 0x7FFFFFFF) - (1 << 32)

VMEM_LIMIT = 56 * 1024 * 1024


def _sigmoid(x):
    return 1.0 / (1.0 + jnp.exp(-x))


def _silu(x):
    return x * _sigmoid(x)


def _matmul_kernel(x_ref, w_ref, o_ref, xb_ref):
    @pl.when(pl.program_id(1) == 0)
    def _():
        xb_ref[...] = x_ref[...].astype(BF16)

    o_ref[...] = jnp.dot(xb_ref[...], w_ref[...], preferred_element_type=F32).astype(o_ref.dtype)


def _matmul(x, w, tm, tn, name):
    m, k = x.shape
    n = w.shape[1]
    return pl.pallas_call(
        _matmul_kernel,
        grid=(m // tm, n // tn),
        in_specs=[pl.BlockSpec((tm, k), lambda i, j: (i, 0)),
                  pl.BlockSpec((k, tn), lambda i, j: (0, j))],
        out_specs=pl.BlockSpec((tm, tn), lambda i, j: (i, j)),
        out_shape=jax.ShapeDtypeStruct((m, n), BF16),
        scratch_shapes=[pltpu.VMEM((tm, k), BF16)],
        compiler_params=pltpu.CompilerParams(
            dimension_semantics=("parallel", "arbitrary"), vmem_limit_bytes=VMEM_LIMIT),
        name=name,
    )(x, w)


def _ssm_kernel(u_ref, g_ref, bt_ref, cre_ref, cim_ref, abar_ref, d_ref, wglu_ref, o_ref,
                st_ref, bu_ref, *, nb, lc):
    rows = nb * lc
    n_re = SSM_COMPLEX

    @pl.when(pl.program_id(0) == 0)
    def _():
        st_ref[...] = jnp.zeros_like(st_ref)

    n_ct = n_re // LANES
    u = pltpu.einshape("btc->tbc", u_ref[...]).reshape(rows, GW)
    for jt in range(2 * n_re // 256):
        k0 = LANES * ((jt % (n_re // 256)) // 2)
        res = jnp.dot(u[:, k0:k0 + LANES], bt_ref[jt], preferred_element_type=F32)
        bu_ref[2 * jt] = res[:, :LANES]
        bu_ref[2 * jt + 1] = res[:, LANES:]

    slab = 4
    for ch in range(n_ct // slab):
        tiles = tuple(range(slab * ch, slab * (ch + 1)))
        a_re = [jnp.broadcast_to(abar_ref[:, LANES * c:LANES * (c + 1)], (nb, LANES)) for c in tiles]
        a_im = [jnp.broadcast_to(abar_ref[:, n_re + LANES * c:n_re + LANES * (c + 1)], (nb, LANES))
                for c in tiles]

        def step(t, carry, tiles=tiles, a_re=a_re, a_im=a_im):
            r = pl.ds(pl.multiple_of(t * nb, nb), nb)
            new = []
            for n, c in enumerate(tiles):
                h_re, h_im = carry[n]
                n_hre = a_re[n] * h_re - a_im[n] * h_im + bu_ref[c, r, :]
                n_him = a_re[n] * h_im + a_im[n] * h_re + bu_ref[n_ct + c, r, :]
                bu_ref[c, r, :] = n_hre
                bu_ref[n_ct + c, r, :] = n_him
                new.append((n_hre, n_him))
            return tuple(new)

        init = tuple((st_ref[c], st_ref[n_ct + c]) for c in tiles)
        final = lax.fori_loop(0, lc, step, init)
        for n, c in enumerate(tiles):
            st_ref[c] = final[n][0]
            st_ref[n_ct + c] = final[n][1]

    ys = []
    for m in range(GW // LANES):
        s_re = jnp.concatenate([bu_ref[4 * m + n] for n in range(4)], axis=1).astype(BF16)
        s_im = jnp.concatenate([bu_ref[n_ct + 4 * m + n] for n in range(4)], axis=1).astype(BF16)
        ys.append(jnp.dot(s_re, cre_ref[m], preferred_element_type=F32)
                  + jnp.dot(s_im, cim_ref[m], preferred_element_type=F32))
    y = jnp.concatenate(ys, axis=1) + d_ref[...] * u.astype(F32)
    z = jnp.dot(y.astype(BF16), wglu_ref[...], preferred_element_type=F32)
    glu = pltpu.einshape("tbc->btc", (z[:, :GW] * _sigmoid(z[:, GW:])).reshape(lc, nb, GW))
    o_ref[...] = (glu * _silu(g_ref[...].astype(F32))).astype(o_ref.dtype)


def _ssm_params(lam_re, lam_im, log_dt, b_re, b_im, c_re, c_im):
    lr = lam_re.astype(F32)
    li = lam_im.astype(F32)
    dt = jnp.exp(log_dt.astype(F32))[:, None]
    mag = jnp.exp(lr * dt)
    abar_re = mag * jnp.cos(li * dt)
    abar_im = mag * jnp.sin(li * dt)
    den = lr * lr + li * li
    nr = abar_re - 1.0
    f_re = (nr * lr + abar_im * li) / den
    f_im = (abar_im * lr - nr * li) / den
    br = b_re.astype(F32)
    bi = b_im.astype(F32)
    bbar_re = f_re[..., None] * br - f_im[..., None] * bi
    bbar_im = f_re[..., None] * bi + f_im[..., None] * br
    eye = jnp.eye(SSM_GROUPS, dtype=F32)
    bfull = jnp.concatenate(
        [jnp.einsum('gnc,gh->gchn', b, eye).reshape(GW, SSM_COMPLEX) for b in (bbar_re, bbar_im)], axis=1)
    n_tiles = SSM_COMPLEX // 256
    bt = jnp.stack([
        bfull[LANES * ((jt % n_tiles) // 2):LANES * ((jt % n_tiles) // 2 + 1), 256 * jt:256 * (jt + 1)]
        for jt in range(2 * n_tiles)]).astype(BF16)
    cfull_re = jnp.einsum('gcn,gh->gnhc', c_re.astype(F32), eye).reshape(SSM_COMPLEX, GW)
    cfull_im = -jnp.einsum('gcn,gh->gnhc', c_im.astype(F32), eye).reshape(SSM_COMPLEX, GW)
    cre = jnp.stack([cfull_re[512 * m:512 * (m + 1), LANES * m:LANES * (m + 1)]
                     for m in range(GW // LANES)]).astype(BF16)
    cim = jnp.stack([cfull_im[512 * m:512 * (m + 1), LANES * m:LANES * (m + 1)]
                     for m in range(GW // LANES)]).astype(BF16)
    abar = jnp.concatenate([abar_re.reshape(1, SSM_COMPLEX), abar_im.reshape(1, SSM_COMPLEX)], axis=1)
    return bt, cre, cim, abar


def _ssm_call(proj3, bt, cre, cim, abar, d_skip, w_glu, lc):
    nb, seq, _ = proj3.shape
    kern = functools.partial(_ssm_kernel, nb=nb, lc=lc)
    const = lambda *shape: pl.BlockSpec(shape, lambda c: (0,) * len(shape))
    return pl.pallas_call(
        kern,
        grid=(seq // lc,),
        in_specs=[pl.BlockSpec((nb, lc, GW), lambda c: (0, c, COL_U)),
                  pl.BlockSpec((nb, lc, GW), lambda c: (0, c, COL_GA)),
                  const(*bt.shape), const(*cre.shape), const(*cim.shape), const(*abar.shape),
                  const(1, GW), const(GW, 2 * GW)],
        out_specs=pl.BlockSpec((nb, lc, GW), lambda c: (0, c, 0)),
        out_shape=jax.ShapeDtypeStruct((nb, seq, GW), BF16),
        scratch_shapes=[pltpu.VMEM((2 * SSM_COMPLEX // LANES, nb, LANES), F32),
                        pltpu.VMEM((2 * SSM_COMPLEX // LANES, nb * lc, LANES), F32)],
        compiler_params=pltpu.CompilerParams(
            dimension_semantics=("arbitrary",), vmem_limit_bytes=VMEM_LIMIT),
        name="ssm_group",
    )(proj3, proj3, bt, cre, cim, abar, d_skip, w_glu)


def _conv_kernel(val_ref, glu_ref, g_ref, cw_ref, cb_ref, lg_ref, lb_ref, wpw_ref, o_ref, h_ref, *, tc):
    @pl.when(pl.program_id(1) == 0)
    def _():
        h_ref[0:CONV_HALO, :] = jnp.zeros((CONV_HALO, GW), F32)

    @pl.when(pl.program_id(1) != 0)
    def _():
        h_ref[0:CONV_HALO, :] = h_ref[tc:tc + CONV_HALO, :]

    val = val_ref[0].astype(F32)
    h_ref[CONV_HALO:CONV_HALO + tc, :] = val * _sigmoid(glu_ref[0].astype(F32))

    rc = 64
    first = CONV_HALO - (CONV_WIDTH - 1)
    for r0 in range(0, tc, rc):
        acc = jnp.zeros((rc, GW), F32)
        for k in range(CONV_WIDTH):
            acc = acc + cw_ref[k:k + 1, :] * h_ref[r0 + first + k:r0 + first + k + rc, :]
        hc = acc + cb_ref[...]
        mu = jnp.mean(hc, axis=-1, keepdims=True)
        xc = hc - mu
        var = jnp.mean(xc * xc, axis=-1, keepdims=True)
        hn = xc * lax.rsqrt(var + LN_EPS) * lg_ref[...] + lb_ref[...]
        hs = _silu(hn)
        y = jnp.dot(hs.astype(BF16), wpw_ref[...], preferred_element_type=F32)
        o_ref[0, r0:r0 + rc, :] = (y * _silu(g_ref[0, r0:r0 + rc, :].astype(F32))).astype(o_ref.dtype)


def _conv_call(proj3, cw, cb, lg, lb, wpw, tc):
    nb, seq, _ = proj3.shape
    kern = functools.partial(_conv_kernel, tc=tc)
    const = lambda *shape: pl.BlockSpec(shape, lambda b, s: (0,) * len(shape))
    return pl.pallas_call(
        kern,
        grid=(nb, seq // tc),
        in_specs=[pl.BlockSpec((1, tc, GW), lambda b, s: (b, s, COL_VAL)),
                  pl.BlockSpec((1, tc, GW), lambda b, s: (b, s, COL_GLU)),
                  pl.BlockSpec((1, tc, GW), lambda b, s: (b, s, COL_GB)),
                  const(CONV_HALO, GW), const(1, GW), const(1, GW), const(1, GW), const(GW, GW)],
        out_specs=pl.BlockSpec((1, tc, GW), lambda b, s: (b, s, 0)),
        out_shape=jax.ShapeDtypeStruct((nb, seq, GW), BF16),
        scratch_shapes=[pltpu.VMEM((tc + CONV_HALO, GW), F32)],
        compiler_params=pltpu.CompilerParams(
            dimension_semantics=("parallel", "arbitrary"), vmem_limit_bytes=VMEM_LIMIT),
        name="conv_group",
    )(proj3, proj3, proj3, cw, cb, lg, lb, wpw)


def _dsa_kernel(q_ref, g_ref, iq_ref, iw_ref, k_ref, v_ref, ika_ref, ikb_ref, bias_ref, o_ref,
                key_scr, half_scr, iwb_scr, tau_scr, cand_scr, cnt_scr, tauw_scr, m_scr, acc_scr,
                *, tb, topk, idx_bits):
    i = pl.program_id(1)
    nkb = i + 1
    halves = tb // LANES
    nt = (((1,), (1,)), ((), ()))
    ones = jnp.ones((LANES, LANES), BF16)

    iw = iw_ref[0].astype(F32)
    for h in range(IDX_HEADS):
        iwb_scr[h] = jnp.broadcast_to(iw[:, h:h + 1], (tb, tb))
    t_idx = i * tb + lax.broadcasted_iota(I32, (tb, tb), 0)
    lane_idx = lax.broadcasted_iota(I32, (tb, tb), 1)

    def score_body(j, carry):
        off = pl.multiple_of(j * tb, tb)
        ka = ika_ref[0, pl.ds(off, tb), :]
        kb = ikb_ref[0, pl.ds(off, tb), :]
        sc = jnp.zeros((tb, tb), F32)
        for p in range(IDX_HEADS // 2):
            iqp = iq_ref[0, :, LANES * p:LANES * (p + 1)]
            da = lax.dot_general(iqp, ka, nt, preferred_element_type=F32)
            db = lax.dot_general(iqp, kb, nt, preferred_element_type=F32)
            sc = sc + iwb_scr[2 * p] * jnp.maximum(da, 0.0) + iwb_scr[2 * p + 1] * jnp.maximum(db, 0.0)
        sc = jnp.where(off + lane_idx <= t_idx, sc, -jnp.inf)
        bits = pltpu.bitcast(sc, I32)
        key = bits ^ ((bits >> 31) & 0x7FFFFFFF)
        key = jnp.where(key == -1, 0, key)
        key_scr[j] = key
        half_scr[j] = (key >> 16).astype(jnp.int16)
        return carry

    lax.fori_loop(0, nkb, score_body, 0)

    def lane_total(part):
        return jnp.dot(part, ones, preferred_element_type=F32)

    def count16():
        cnt_scr[...] = jnp.zeros((tb, LANES), BF16)

        def body(j, carry):
            kt = half_scr[j]
            c = cand_scr[...]
            part = cnt_scr[...]
            for hf in range(halves):
                part = part + jnp.where(kt[:, LANES * hf:LANES * (hf + 1)] >= c,
                                        jnp.ones((), BF16), jnp.zeros((), BF16))
            cnt_scr[...] = part
            return carry

        lax.fori_loop(0, nkb, body, 0)
        return lane_total(cnt_scr[...])

    def count32(pred):
        def body(j, part):
            kt = key_scr[j]
            for hf in range(halves):
                part = part + jnp.where(pred(kt[:, LANES * hf:LANES * (hf + 1)], j * tb + LANES * hf), 1.0, 0.0)
            return part

        return lane_total(lax.fori_loop(0, nkb, body, jnp.zeros((tb, LANES), F32)).astype(BF16))

    def search16(target):
        tau_scr[...] = jnp.zeros((tb, LANES), I32)

        def body(it, carry):
            cand_u = tau_scr[...] | lax.shift_left(jnp.int32(1), 15 - it)
            cand_scr[...] = (cand_u - 32768).astype(jnp.int16)
            tau_scr[...] = jnp.where(count16() >= target, cand_u, tau_scr[...])
            return carry

        lax.fori_loop(0, 16, body, 0)
        return tau_scr[...] - 32768

    tau_hi = search16(topk)
    cand_scr[...] = (tau_hi + 1).astype(jnp.int16)
    need_lo = topk - count16()

    def low_body(j, carry):
        kt = key_scr[j]
        same = (kt >> 16) == jnp.concatenate([tau_hi] * halves, axis=1)
        half_scr[j] = jnp.where(same, (kt & 0xFFFF) - 32768, -32768).astype(jnp.int16)
        return carry

    lax.fori_loop(0, nkb, low_body, 0)
    tau_lo = search16(need_lo)
    tau = jnp.maximum(tau_hi * 65536 + (tau_lo + 32768), KEY_NEG_INF + 1)

    n_ge = count32(lambda kt, base: kt >= tau)
    any_excess = jnp.max(jnp.where(n_ge > topk, 1.0, 0.0)) > 0.0

    @pl.when(any_excess)
    def _():
        lane128 = lax.broadcasted_iota(I32, (tb, LANES), 1)
        need = topk - count32(lambda kt, base: kt > tau)
        tau_scr[...] = jnp.zeros((tb, LANES), I32)

        def idx_body(it, carry):
            cand_m = tau_scr[...] | lax.shift_left(jnp.int32(1), idx_bits - 1 - it)
            cnt = count32(lambda kt, base: (kt == tau) & (base + lane128 < cand_m))
            tau_scr[...] = jnp.where(cnt < need, cand_m, tau_scr[...])
            return carry

        lax.fori_loop(0, idx_bits, idx_body, 0)
        last = tau_scr[...]

        def demote_body(j, carry):
            kt = key_scr[j]
            parts = []
            for hf in range(halves):
                kh = kt[:, LANES * hf:LANES * (hf + 1)]
                drop = (kh == tau) & (j * tb + LANES * hf + lane128 > last)
                parts.append(jnp.where(drop, kh - 1, kh))
            key_scr[j] = jnp.concatenate(parts, axis=1)
            return carry

        lax.fori_loop(0, nkb, demote_body, 0)

    tauw_scr[...] = jnp.concatenate([tau] * halves, axis=1)
    m_scr[...] = jnp.full(m_scr.shape, NEG_BIG, F32)
    acc_scr[...] = jnp.zeros(acc_scr.shape, F32)
    c_log2 = HEAD_DIM ** -0.5 * LOG2_E
    ones_v = jnp.ones((tb, HEAD_DIM), BF16)

    def att_body(j, carry):
        off = pl.multiple_of(j * tb, tb)
        sel = key_scr[j] >= tauw_scr[...]
        bsel = jnp.minimum(i - j, 2)
        for h in range(ATT_HEADS):
            c0 = HEAD_DIM * h
            qh = q_ref[0, :, c0:c0 + HEAD_DIM]
            kh = k_ref[0, pl.ds(off, tb), c0:c0 + HEAD_DIM]
            vh = jnp.concatenate([v_ref[0, pl.ds(off, tb), c0:c0 + HEAD_DIM], ones_v], axis=1)
            s = lax.dot_general(qh, kh, nt, preferred_element_type=F32) * c_log2 + bias_ref[bsel, h]
            s = jnp.where(sel, s, NEG_BIG)
            m_prev = m_scr[h]
            s_max = s[:, 0:LANES]
            for hf in range(1, halves):
                s_max = jnp.maximum(s_max, s[:, LANES * hf:LANES * (hf + 1)])
            m_new = jnp.maximum(m_prev, jnp.max(s_max, axis=1, keepdims=True))
            alpha = jnp.exp2(m_prev - m_new)
            p = jnp.concatenate([jnp.exp2(s[:, LANES * hf:LANES * (hf + 1)] - m_new) for hf in range(halves)],
                                axis=1).astype(BF16)
            acc_scr[h] = (jnp.concatenate([alpha, alpha], axis=1) * acc_scr[h]
                          + jnp.dot(p, vh, preferred_element_type=F32))
            m_scr[h] = m_new
        return carry

    lax.fori_loop(0, nkb, att_body, 0)

    y = jnp.concatenate([acc_scr[h][:, :HEAD_DIM] / acc_scr[h][:, HEAD_DIM:] for h in range(ATT_HEADS)], axis=1)
    o_ref[0] = (y * _silu(g_ref[0].astype(F32))).astype(o_ref.dtype)


def _rel_bucket(dist):
    n = jnp.maximum(dist, 0)
    max_exact = REL_BUCKETS // 2
    nf = jnp.maximum(n, 1).astype(F32)
    large = max_exact + (jnp.log(nf / max_exact) / math.log(REL_MAX_DIST / max_exact)
                         * (REL_BUCKETS - max_exact)).astype(I32)
    large = jnp.minimum(large, REL_BUCKETS - 1)
    return jnp.where(n < max_exact, n, large)


def _bias_tiles(rel_bias, tb):
    assert tb >= REL_MAX_DIST
    q = jnp.arange(tb, dtype=I32)[:, None]
    s = jnp.arange(tb, dtype=I32)[None, :]
    bucket = jnp.stack([_rel_bucket(delta + q - s) for delta in (0, tb, 2 * tb)])[:, None]
    table = rel_bias.astype(F32) * LOG2_E
    tiles = jnp.zeros((3, ATT_HEADS, tb, tb), F32)
    for b in range(REL_BUCKETS):
        tiles = jnp.where(bucket == b, table[b][None, :, None, None], tiles)
    return tiles


def _dsa_call(proj3, bias, tb, topk):
    nb, seq, _ = proj3.shape
    nkb = seq // tb
    kern = functools.partial(_dsa_kernel, tb=tb, topk=float(topk), idx_bits=max(1, (seq - 1).bit_length()))
    qblk = lambda col: pl.BlockSpec((1, tb, GW), lambda b, i: (b, i, col))
    return pl.pallas_call(
        kern,
        grid=(nb, nkb),
        in_specs=[qblk(COL_Q), qblk(COL_GC), qblk(COL_IQ),
                  pl.BlockSpec((1, tb, LANES), lambda b, i: (b, i, COL_IW)),
                  pl.BlockSpec((1, seq, GW), lambda b, i: (b, 0, COL_K)),
                  pl.BlockSpec((1, seq, GW), lambda b, i: (b, 0, COL_V)),
                  pl.BlockSpec((1, seq, LANES), lambda b, i: (b, 0, COL_IKA)),
                  pl.BlockSpec((1, seq, LANES), lambda b, i: (b, 0, COL_IKB)),
                  pl.BlockSpec(bias.shape, lambda b, i: (0, 0, 0, 0))],
        out_specs=pl.BlockSpec((1, tb, GW), lambda b, i: (b, i, 0)),
        out_shape=jax.ShapeDtypeStruct((nb, seq, GW), BF16),
        scratch_shapes=[pltpu.VMEM((nkb, tb, tb), I32),
                        pltpu.VMEM((nkb, tb, tb), jnp.int16),
                        pltpu.VMEM((IDX_HEADS, tb, tb), F32),
                        pltpu.VMEM((tb, LANES), I32),
                        pltpu.VMEM((tb, LANES), jnp.int16),
                        pltpu.VMEM((tb, LANES), BF16),
                        pltpu.VMEM((tb, tb), I32),
                        pltpu.VMEM((ATT_HEADS, tb, LANES), F32),
                        pltpu.VMEM((ATT_HEADS, tb, 2 * HEAD_DIM), F32)],
        compiler_params=pltpu.CompilerParams(
            dimension_semantics=("parallel", "arbitrary"), vmem_limit_bytes=VMEM_LIMIT),
        name="dsa_group",
    )(proj3, proj3, proj3, proj3, proj3, proj3, proj3, proj3, bias)


def _mem_kernel(q_ref, g_ref, kv_ref, o_ref):
    nt = (((1,), (1,)), ((), ()))
    scale = HEAD_DIM ** -0.5
    outs = []
    for h in range(MEM_HEADS):
        c0 = HEAD_DIM * h
        qh = q_ref[0, :, c0:c0 + HEAD_DIM]
        kh = kv_ref[0, :, c0:c0 + HEAD_DIM]
        vh = kv_ref[0, :, GW + c0:GW + c0 + HEAD_DIM]
        s = lax.dot_general(qh, kh, nt, preferred_element_type=F32) * scale
        m = jnp.max(s, axis=1, keepdims=True)
        p = jnp.exp(s - m)
        l = jnp.sum(p, axis=1, keepdims=True)
        outs.append(jnp.dot(p.astype(BF16), vh, preferred_element_type=F32) / l)
    y = jnp.concatenate(outs, axis=1)
    o_ref[0] = (y * _silu(g_ref[0].astype(F32))).astype(o_ref.dtype)


def _mem_call(proj3, mkv, tm):
    nb, seq, _ = proj3.shape
    n_mem = mkv.shape[1]
    return pl.pallas_call(
        _mem_kernel,
        grid=(nb, seq // tm),
        in_specs=[pl.BlockSpec((1, tm, GW), lambda b, s: (b, s, COL_MQ)),
                  pl.BlockSpec((1, tm, GW), lambda b, s: (b, s, COL_GD)),
                  pl.BlockSpec((1, n_mem, 2 * GW), lambda b, s: (b, 0, 0))],
        out_specs=pl.BlockSpec((1, tm, GW), lambda b, s: (b, s, 0)),
        out_shape=jax.ShapeDtypeStruct((nb, seq, GW), BF16),
        compiler_params=pltpu.CompilerParams(
            dimension_semantics=("parallel", "parallel"), vmem_limit_bytes=VMEM_LIMIT),
        name="mem_group",
    )(proj3, proj3, mkv)


def _out_kernel(ya_ref, yb_ref, yc_ref, yd_ref, x_ref, w_ref, g_ref, b_ref, o_ref, *, alpha):
    acc = alpha * x_ref[...]
    for n, y_ref in enumerate((ya_ref, yb_ref, yc_ref, yd_ref)):
        acc = acc + jnp.dot(y_ref[...], w_ref[GW * n:GW * (n + 1), :], preferred_element_type=F32)
    mu = jnp.mean(acc, axis=-1, keepdims=True)
    xc = acc - mu
    var = jnp.mean(xc * xc, axis=-1, keepdims=True)
    o_ref[...] = xc * lax.rsqrt(var + LN_EPS) * g_ref[...] + b_ref[...]


def _out_call(ys, x2, w_out, ln_g, ln_b, alpha, tm):
    t = x2.shape[0]
    kern = functools.partial(_out_kernel, alpha=alpha)
    row = lambda width: pl.BlockSpec((tm, width), lambda r: (r, 0))
    const = lambda *shape: pl.BlockSpec(shape, lambda r: (0,) * len(shape))
    return pl.pallas_call(
        kern,
        grid=(t // tm,),
        in_specs=[row(GW), row(GW), row(GW), row(GW), row(D_MODEL),
                  const(4 * GW, D_MODEL), const(1, D_MODEL), const(1, D_MODEL)],
        out_specs=row(D_MODEL),
        out_shape=jax.ShapeDtypeStruct((t, D_MODEL), F32),
        compiler_params=pltpu.CompilerParams(
            dimension_semantics=("parallel",), vmem_limit_bytes=VMEM_LIMIT),
        name="out_deepnorm",
    )(*ys, x2, w_out, ln_g, ln_b)


def _pack_w_in(w):
    d = w.shape[0]
    ik0 = 9 * GW + IDX_HEADS * IDX_DIM
    iw0 = ik0 + IDX_DIM
    md0 = iw0 + IDX_HEADS
    w_ik = w[:, ik0:iw0]
    zeros = lambda n: jnp.zeros((d, n), w.dtype)
    packed = jnp.concatenate(
        [w[:, :ik0], w[:, md0:md0 + 2 * GW],
         w_ik, zeros(LANES - IDX_DIM),
         zeros(LANES - IDX_DIM), w_ik,
         w[:, iw0:md0], zeros(LANES - IDX_HEADS),
         zeros(PACKED_WIDTH - 12 * GW - 3 * LANES)], axis=1)
    return packed.astype(BF16)


def _pick(n, pref):
    while n % pref:
        pref //= 2
    return pref


def kernel(x, mem, w_in, ssm_lam_re, ssm_lam_im, ssm_log_dt, ssm_b_re, ssm_b_im, ssm_c_re, ssm_c_im,
           ssm_d, ssm_w_glu, conv_w, conv_b, conv_ln_g, conv_ln_b, conv_w_pw, rel_bias, mem_w_kv,
           w_out, ln_g, ln_b):
    nb, seq, d = x.shape
    depth = w_in.shape[0]
    t = nb * seq
    alpha = (2 * depth) ** 0.25
    topk = min(TOPK_MAX, seq // 4)
    tb = 256
    tm = _pick(t, 1024)
    n_mem = mem.shape[1]

    bias = _bias_tiles(rel_bias, tb)
    mem2 = mem.reshape(nb * n_mem, d)
    x2 = x.reshape(t, d)
    for l in range(depth):
        proj = _matmul(x2, _pack_w_in(w_in[l]), tm, GW, "in_proj")
        proj3 = proj.reshape(nb, seq, PACKED_WIDTH)
        mkv = _matmul(mem2, mem_w_kv[l].astype(BF16), _pick(nb * n_mem, 1024), GW, "mem_kv")
        mkv = mkv.reshape(nb, n_mem, 2 * GW)

        bt, cre, cim, abar = _ssm_params(ssm_lam_re[l], ssm_lam_im[l], ssm_log_dt[l], ssm_b_re[l],
                                         ssm_b_im[l], ssm_c_re[l], ssm_c_im[l])
        y_a = _ssm_call(proj3, bt, cre, cim, abar, ssm_d[l].reshape(1, GW).astype(F32),
                        ssm_w_glu[l].astype(BF16), _pick(seq, 128))
        cw = jnp.concatenate([conv_w[l].reshape(CONV_WIDTH, GW).astype(F32),
                              jnp.zeros((CONV_HALO - CONV_WIDTH, GW), F32)], axis=0)
        y_b = _conv_call(proj3, cw, conv_b[l].reshape(1, GW).astype(F32),
                         conv_ln_g[l].reshape(1, GW).astype(F32), conv_ln_b[l].reshape(1, GW).astype(F32),
                         conv_w_pw[l].astype(BF16), _pick(seq, 512))
        y_c = _dsa_call(proj3, bias, tb, topk)
        y_d = _mem_call(proj3, mkv, _pick(seq, 512))

        ys = [y.reshape(t, GW) for y in (y_a, y_b, y_c, y_d)]
        x2 = _out_call(ys, x2, w_out[l].astype(BF16), ln_g[l].reshape(1, d).astype(F32),
                       ln_b[l].reshape(1, d).astype(F32), alpha, _pick(t, 512))
    return x2.reshape(nb, seq, d)
```

```python
import functools
import math

import jax
import jax.numpy as jnp
import numpy as np
from jax import lax
from jax.experimental import pallas as pl
from jax.experimental.pallas import tpu as pltpu

F32 = jnp.float32
BF16 = jnp.bfloat16
I32 = jnp.int32
I16 = jnp.int16

D_MODEL = 2048
GW = 512
SSM_CH = 16
SSM_GROUPS = GW // SSM_CH
SSM_STATE = 64
SSM_COMPLEX = SSM_GROUPS * SSM_STATE
CONV_WIDTH = 31
CONV_HALO = 32
ATT_HEADS = 4
HEAD_DIM = GW // ATT_HEADS
IDX_HEADS = 8
IDX_DIM = 64
TOPK_MAX = 256
REL_BUCKETS = 32
REL_MAX_DIST = 128
MEM_HEADS = 4
LN_EPS = 1e-5
NEG_BIG = -1e30
LOG2_E = math.log2(math.e)

LANES = 128
PACK16 = 16
COL_U, COL_GA, COL_VAL, COL_GLU, COL_GB, COL_Q, COL_K, COL_V, COL_GC, COL_IQ, COL_MQ, COL_GD = range(12)
COL_IKA = 12 * (GW // LANES)
COL_IKB = COL_IKA + 1
COL_IW = COL_IKA + 2
PACKED_WIDTH = 13 * GW

KEY_NEG_INF = int(np.array(0xFF800000, np.uint32).astype(np.int64) ^ 0x7FFFFFFF) - (1 << 32)
HALF_OFFSET = 1 << 15

VMEM_LIMIT = 56 * 1024 * 1024


def _sigmoid(x):
    return 1.0 / (1.0 + jnp.exp(-x))


def _silu(x):
    return x * _sigmoid(x)


def _matmul_kernel(x_ref, w_ref, o_ref, xb_ref):
    @pl.when(pl.program_id(1) == 0)
    def _():
        xb_ref[...] = x_ref[...].astype(BF16)

    o_ref[...] = jnp.dot(xb_ref[...], w_ref[...], preferred_element_type=F32).astype(o_ref.dtype)


def _matmul(x, w, tm, tn, name):
    m, k = x.shape
    n = w.shape[1]
    return pl.pallas_call(
        _matmul_kernel,
        grid=(m // tm, n // tn),
        in_specs=[pl.BlockSpec((tm, k), lambda i, j: (i, 0)),
                  pl.BlockSpec((k, tn), lambda i, j: (0, j))],
        out_specs=pl.BlockSpec((tm, tn), lambda i, j: (i, j)),
        out_shape=jax.ShapeDtypeStruct((m, n), BF16),
        scratch_shapes=[pltpu.VMEM((tm, k), BF16)],
        compiler_params=pltpu.CompilerParams(
            dimension_semantics=("parallel", "arbitrary"), vmem_limit_bytes=VMEM_LIMIT),
        name=name,
    )(x, w)


def _ssm_kernel(u_ref, g_ref, bt_ref, cre_ref, cim_ref, abar_ref, d_ref, wglu_ref, o_ref,
                st_ref, bu_ref, *, nb, lc):
    rows = nb * lc
    n_re = SSM_COMPLEX

    @pl.when(pl.program_id(0) == 0)
    def _():
        st_ref[...] = jnp.zeros_like(st_ref)

    n_ct = n_re // LANES
    u = pltpu.einshape("btc->tbc", u_ref[...]).reshape(rows, GW)
    for jt in range(2 * n_re // 256):
        k0 = LANES * ((jt % (n_re // 256)) // 2)
        res = jnp.dot(u[:, k0:k0 + LANES], bt_ref[jt], preferred_element_type=F32)
        bu_ref[2 * jt] = res[:, :LANES]
        bu_ref[2 * jt + 1] = res[:, LANES:]

    slab = 4
    for ch in range(n_ct // slab):
        tiles = tuple(range(slab * ch, slab * (ch + 1)))
        a_re = [jnp.broadcast_to(abar_ref[:, LANES * c:LANES * (c + 1)], (nb, LANES)) for c in tiles]
        a_im = [jnp.broadcast_to(abar_ref[:, n_re + LANES * c:n_re + LANES * (c + 1)], (nb, LANES))
                for c in tiles]

        def step(t, carry, tiles=tiles, a_re=a_re, a_im=a_im):
            r = pl.ds(pl.multiple_of(t * nb, nb), nb)
            new = []
            for n, c in enumerate(tiles):
                h_re, h_im = carry[n]
                n_hre = a_re[n] * h_re - a_im[n] * h_im + bu_ref[c, r, :]
                n_him = a_re[n] * h_im + a_im[n] * h_re + bu_ref[n_ct + c, r, :]
                bu_ref[c, r, :] = n_hre
                bu_ref[n_ct + c, r, :] = n_him
                new.append((n_hre, n_him))
            return tuple(new)

        init = tuple((st_ref[c], st_ref[n_ct + c]) for c in tiles)
        final = lax.fori_loop(0, lc, step, init)
        for n, c in enumerate(tiles):
            st_ref[c] = final[n][0]
            st_ref[n_ct + c] = final[n][1]

    ys = []
    for m in range(GW // LANES):
        s_re = jnp.concatenate([bu_ref[4 * m + n] for n in range(4)], axis=1).astype(BF16)
        s_im = jnp.concatenate([bu_ref[n_ct + 4 * m + n] for n in range(4)], axis=1).astype(BF16)
        ys.append(jnp.dot(s_re, cre_ref[m], preferred_element_type=F32)
                  + jnp.dot(s_im, cim_ref[m], preferred_element_type=F32))
    y = jnp.concatenate(ys, axis=1) + d_ref[...] * u.astype(F32)
    z = jnp.dot(y.astype(BF16), wglu_ref[...], preferred_element_type=F32)
    glu = pltpu.einshape("tbc->btc", (z[:, :GW] * _sigmoid(z[:, GW:])).reshape(lc, nb, GW))
    o_ref[...] = (glu * _silu(g_ref[...].astype(F32))).astype(o_ref.dtype)


def _ssm_params(lam_re, lam_im, log_dt, b_re, b_im, c_re, c_im):
    lr = lam_re.astype(F32)
    li = lam_im.astype(F32)
    dt = jnp.exp(log_dt.astype(F32))[:, None]
    mag = jnp.exp(lr * dt)
    abar_re = mag * jnp.cos(li * dt)
    abar_im = mag * jnp.sin(li * dt)
    den = lr * lr + li * li
    nr = abar_re - 1.0
    f_re = (nr * lr + abar_im * li) / den
    f_im = (abar_im * lr - nr * li) / den
    br = b_re.astype(F32)
    bi = b_im.astype(F32)
    bbar_re = f_re[..., None] * br - f_im[..., None] * bi
    bbar_im = f_re[..., None] * bi + f_im[..., None] * br
    eye = jnp.eye(SSM_GROUPS, dtype=F32)
    bfull = jnp.concatenate(
        [jnp.einsum('gnc,gh->gchn', b, eye).reshape(GW, SSM_COMPLEX) for b in (bbar_re, bbar_im)], axis=1)
    n_tiles = SSM_COMPLEX // 256
    bt = jnp.stack([
        bfull[LANES * ((jt % n_tiles) // 2):LANES * ((jt % n_tiles) // 2 + 1), 256 * jt:256 * (jt + 1)]
        for jt in range(2 * n_tiles)]).astype(BF16)
    cfull_re = jnp.einsum('gcn,gh->gnhc', c_re.astype(F32), eye).reshape(SSM_COMPLEX, GW)
    cfull_im = -jnp.einsum('gcn,gh->gnhc', c_im.astype(F32), eye).reshape(SSM_COMPLEX, GW)
    cre = jnp.stack([cfull_re[512 * m:512 * (m + 1), LANES * m:LANES * (m + 1)]
                     for m in range(GW // LANES)]).astype(BF16)
    cim = jnp.stack([cfull_im[512 * m:512 * (m + 1), LANES * m:LANES * (m + 1)]
                     for m in range(GW // LANES)]).astype(BF16)
    abar = jnp.concatenate([abar_re.reshape(1, SSM_COMPLEX), abar_im.reshape(1, SSM_COMPLEX)], axis=1)
    return bt, cre, cim, abar


def _ssm_call(proj3, bt, cre, cim, abar, d_skip, w_glu, lc):
    nb, seq, _ = proj3.shape
    kern = functools.partial(_ssm_kernel, nb=nb, lc=lc)
    const = lambda *shape: pl.BlockSpec(shape, lambda c: (0,) * len(shape))
    return pl.pallas_call(
        kern,
        grid=(seq // lc,),
        in_specs=[pl.BlockSpec((nb, lc, GW), lambda c: (0, c, COL_U)),
                  pl.BlockSpec((nb, lc, GW), lambda c: (0, c, COL_GA)),
                  const(*bt.shape), const(*cre.shape), const(*cim.shape), const(*abar.shape),
                  const(1, GW), const(GW, 2 * GW)],
        out_specs=pl.BlockSpec((nb, lc, GW), lambda c: (0, c, 0)),
        out_shape=jax.ShapeDtypeStruct((nb, seq, GW), BF16),
        scratch_shapes=[pltpu.VMEM((2 * SSM_COMPLEX // LANES, nb, LANES), F32),
                        pltpu.VMEM((2 * SSM_COMPLEX // LANES, nb * lc, LANES), F32)],
        compiler_params=pltpu.CompilerParams(
            dimension_semantics=("arbitrary",), vmem_limit_bytes=VMEM_LIMIT),
        name="ssm_group",
    )(proj3, proj3, bt, cre, cim, abar, d_skip, w_glu)


def _conv_kernel(val_ref, glu_ref, g_ref, cw_ref, cb_ref, lg_ref, lb_ref, wpw_ref, o_ref, h_ref, *, tc):
    @pl.when(pl.program_id(1) == 0)
    def _():
        h_ref[0:CONV_HALO, :] = jnp.zeros((CONV_HALO, GW), F32)

    @pl.when(pl.program_id(1) != 0)
    def _():
        h_ref[0:CONV_HALO, :] = h_ref[tc:tc + CONV_HALO, :]

    val = val_ref[0].astype(F32)
    h_ref[CONV_HALO:CONV_HALO + tc, :] = val * _sigmoid(glu_ref[0].astype(F32))

    rc = 64
    first = CONV_HALO - (CONV_WIDTH - 1)
    for r0 in range(0, tc, rc):
        acc = jnp.zeros((rc, GW), F32)
        for k in range(CONV_WIDTH):
            acc = acc + cw_ref[k:k + 1, :] * h_ref[r0 + first + k:r0 + first + k + rc, :]
        hc = acc + cb_ref[...]
        mu = jnp.mean(hc, axis=-1, keepdims=True)
        xc = hc - mu
        var = jnp.mean(xc * xc, axis=-1, keepdims=True)
        hn = xc * lax.rsqrt(var + LN_EPS) * lg_ref[...] + lb_ref[...]
        hs = _silu(hn)
        y = jnp.dot(hs.astype(BF16), wpw_ref[...], preferred_element_type=F32)
        o_ref[0, r0:r0 + rc, :] = (y * _silu(g_ref[0, r0:r0 + rc, :].astype(F32))).astype(o_ref.dtype)


def _conv_call(proj3, cw, cb, lg, lb, wpw, tc):
    nb, seq, _ = proj3.shape
    kern = functools.partial(_conv_kernel, tc=tc)
    const = lambda *shape: pl.BlockSpec(shape, lambda b, s: (0,) * len(shape))
    return pl.pallas_call(
        kern,
        grid=(nb, seq // tc),
        in_specs=[pl.BlockSpec((1, tc, GW), lambda b, s: (b, s, COL_VAL)),
                  pl.BlockSpec((1, tc, GW), lambda b, s: (b, s, COL_GLU)),
                  pl.BlockSpec((1, tc, GW), lambda b, s: (b, s, COL_GB)),
                  const(CONV_HALO, GW), const(1, GW), const(1, GW), const(1, GW), const(GW, GW)],
        out_specs=pl.BlockSpec((1, tc, GW), lambda b, s: (b, s, 0)),
        out_shape=jax.ShapeDtypeStruct((nb, seq, GW), BF16),
        scratch_shapes=[pltpu.VMEM((tc + CONV_HALO, GW), F32)],
        compiler_params=pltpu.CompilerParams(
            dimension_semantics=("parallel", "arbitrary"), vmem_limit_bytes=VMEM_LIMIT),
        name="conv_group",
    )(proj3, proj3, proj3, cw, cb, lg, lb, wpw)


def _dsa_kernel(q_ref, g_ref, iq_ref, iw_ref, k_ref, v_ref, ika_ref, ikb_ref, bias_ref, o_ref,
                key_scr, half_scr, vt_scr, mask_scr, s_scr, p_scr, m_scr, acc_scr, *, tb, topk, seq):
    i = pl.program_id(1)
    nkb = i + 1
    nt = (((1,), (1,)), ((), ()))
    idx_bits = max(1, (seq - 1).bit_length())
    row_idx = lax.broadcasted_iota(I32, (tb, tb), 0)
    lane_idx = lax.broadcasted_iota(I32, (tb, tb), 1)

    @pl.when(i == 0)
    def _():
        for j in range(seq // tb):
            vt_scr[j] = v_ref[0, j * tb:(j + 1) * tb, :].T

    iw_t = iw_ref[0].astype(F32).T[0:IDX_HEADS, :]

    kh_rows = tb // 2
    part_row = lax.broadcasted_iota(I32, (kh_rows, tb), 0)
    part_lane = lax.broadcasted_iota(I32, (kh_rows, tb), 1)

    def score_body(j, carry):
        for part in range(tb // kh_rows):
            off = pl.multiple_of(j * tb + part * kh_rows, kh_rows)
            ka = ika_ref[0, pl.ds(off, kh_rows), :]
            kb = ikb_ref[0, pl.ds(off, kh_rows), :]
            sc = jnp.zeros((kh_rows, tb), F32)
            for p in range(IDX_HEADS // 2):
                iqp = iq_ref[0, :, LANES * p:LANES * (p + 1)]
                da = lax.dot_general(ka, iqp, nt, preferred_element_type=F32)
                db = lax.dot_general(kb, iqp, nt, preferred_element_type=F32)
                sc = (sc + iw_t[2 * p:2 * p + 1, :] * jnp.maximum(da, 0.0)
                      + iw_t[2 * p + 1:2 * p + 2, :] * jnp.maximum(db, 0.0))
            sc = jnp.where(off + part_row <= i * tb + part_lane, sc, -jnp.inf)
            bits = pltpu.bitcast(sc, I32)
            key = bits ^ ((bits >> 31) & 0x7FFFFFFF)
            key = jnp.where(key == -1, 0, key)
            key_scr[j, part * kh_rows:(part + 1) * kh_rows, :] = key
            half_scr[j, part * kh_rows:(part + 1) * kh_rows, :] = (key >> 16).astype(I16)
        return carry

    lax.fori_loop(0, nkb, score_body, 0)

    one16 = jnp.ones((), BF16)
    zero16 = jnp.zeros((), BF16)

    def count16(cand):
        c16 = jnp.broadcast_to(cand, (PACK16, tb)).astype(I16)

        def body(j, accs):
            a0, a1 = accs
            kt = half_scr[j]
            for r in range(0, tb // PACK16, 2):
                a0 = a0 + jnp.where(kt[PACK16 * r:PACK16 * (r + 1), :] >= c16, one16, zero16)
                a1 = a1 + jnp.where(kt[PACK16 * (r + 1):PACK16 * (r + 2), :] >= c16, one16, zero16)
            return a0, a1

        zeros = jnp.zeros((PACK16, tb), BF16)
        a0, a1 = lax.fori_loop(0, nkb, body, (zeros, zeros))
        return jnp.sum(a0.astype(F32) + a1.astype(F32), axis=0, keepdims=True)

    def search(target, nbits, offset):
        def body(it, state):
            best, above = state
            cand = best | lax.shift_left(jnp.int32(1), nbits - 1 - it)
            cnt = count16(cand - offset)
            ok = cnt >= target
            return jnp.where(ok, cand, best), jnp.where(ok, above, cnt)

        best, above = lax.fori_loop(0, nbits, body, (jnp.zeros((1, tb), I32), jnp.zeros((1, tb), F32)))
        return best - offset, above

    tau_hi, above = search(topk, 16, HALF_OFFSET)
    need = topk - above

    def low_body(j, carry):
        kt = key_scr[j]
        half_scr[j] = jnp.where((kt >> 16) == tau_hi, (kt & 0xFFFF) - HALF_OFFSET, -HALF_OFFSET).astype(I16)
        return carry

    lax.fori_loop(0, nkb, low_body, 0)
    tau_lo, above = search(need, 16, HALF_OFFSET)
    need = need - above
    tau = tau_hi * 65536 + (tau_lo + HALF_OFFSET)

    def tie_body(j, carry):
        half_scr[j] = jnp.where(key_scr[j] == tau, (seq - 1 - j * tb) - row_idx, -1).astype(I16)
        return carry

    lax.fori_loop(0, nkb, tie_body, 0)
    rank, _ = search(need, idx_bits, 0)
    few = tau <= KEY_NEG_INF
    tau = jnp.where(few, KEY_NEG_INF + 1, tau)
    last_tie = jnp.where(few, seq, seq - 1 - rank)

    def demote_body(j, carry):
        kt = key_scr[j]
        drop = (kt == tau) & (j * tb + row_idx > last_tie)
        key_scr[j] = jnp.where(drop, kt - 1, kt)
        return carry

    lax.fori_loop(0, nkb, demote_body, 0)

    m_scr[...] = jnp.full(m_scr.shape, NEG_BIG, F32)
    acc_scr[...] = jnp.zeros(acc_scr.shape, F32)
    c_log2 = HEAD_DIM ** -0.5 * LOG2_E
    ones_rows = jnp.ones((PACK16, tb), BF16)

    def att_body(j, carry):
        off = pl.multiple_of(j * tb, tb)
        mask_scr[...] = jnp.where(key_scr[j] >= tau, 0.0, NEG_BIG)
        bsel = jnp.minimum(i - j, 2)
        for h in range(ATT_HEADS):
            c0 = HEAD_DIM * h
            kh = k_ref[0, pl.ds(off, tb), c0:c0 + HEAD_DIM]
            qh = q_ref[0, :, c0:c0 + HEAD_DIM]
            s_scr[h] = (lax.dot_general(kh, qh, nt, preferred_element_type=F32) * c_log2
                        + bias_ref[bsel, h] + mask_scr[...])
        m_new, alpha = [], []
        for h in range(ATT_HEADS):
            m_prev = m_scr[h]
            m_new.append(jnp.maximum(m_prev, jnp.max(s_scr[h], axis=0, keepdims=True)))
            alpha.append(jnp.exp2(m_prev - m_new[h]))
            m_scr[h] = m_new[h]
        for h in range(ATT_HEADS):
            p_scr[h] = jnp.exp2(s_scr[h] - m_new[h]).astype(BF16)
        for h in range(ATT_HEADS):
            c0 = HEAD_DIM * h
            vt = jnp.concatenate([vt_scr[j, c0:c0 + HEAD_DIM, :], ones_rows], axis=0)
            acc_scr[h] = alpha[h] * acc_scr[h] + jnp.dot(vt, p_scr[h], preferred_element_type=F32)
        return carry

    lax.fori_loop(0, nkb, att_body, 0)

    y = jnp.concatenate(
        [(acc_scr[h, 0:HEAD_DIM, :] / acc_scr[h, HEAD_DIM:HEAD_DIM + 1, :]).T for h in range(ATT_HEADS)], axis=1)
    o_ref[0] = (y * _silu(g_ref[0].astype(F32))).astype(o_ref.dtype)


def _rel_bucket(dist):
    n = jnp.maximum(dist, 0)
    max_exact = REL_BUCKETS // 2
    nf = jnp.maximum(n, 1).astype(F32)
    large = max_exact + (jnp.log(nf / max_exact) / math.log(REL_MAX_DIST / max_exact)
                         * (REL_BUCKETS - max_exact)).astype(I32)
    large = jnp.minimum(large, REL_BUCKETS - 1)
    return jnp.where(n < max_exact, n, large)


def _bias_tiles(rel_bias, tb):
    assert tb >= REL_MAX_DIST
    s = jnp.arange(tb, dtype=I32)[:, None]
    q = jnp.arange(tb, dtype=I32)[None, :]
    bucket = jnp.stack([_rel_bucket(delta + q - s) for delta in (0, tb, 2 * tb)])[:, None]
    table = rel_bias.astype(F32) * LOG2_E
    tiles = jnp.zeros((3, ATT_HEADS, tb, tb), F32)
    for b in range(REL_BUCKETS):
        tiles = jnp.where(bucket == b, table[b][None, :, None, None], tiles)
    return tiles


def _dsa_call(proj3, bias, tb, topk):
    nb, seq, _ = proj3.shape
    nkb = seq // tb
    kern = functools.partial(_dsa_kernel, tb=tb, topk=float(topk), seq=seq)
    qblk = lambda col: pl.BlockSpec((1, tb, GW), lambda b, i: (b, i, col))
    return pl.pallas_call(
        kern,
        grid=(nb, nkb),
        in_specs=[qblk(COL_Q), qblk(COL_GC), qblk(COL_IQ),
                  pl.BlockSpec((1, tb, LANES), lambda b, i: (b, i, COL_IW)),
                  pl.BlockSpec((1, seq, GW), lambda b, i: (b, 0, COL_K)),
                  pl.BlockSpec((1, seq, GW), lambda b, i: (b, 0, COL_V)),
                  pl.BlockSpec((1, seq, LANES), lambda b, i: (b, 0, COL_IKA)),
                  pl.BlockSpec((1, seq, LANES), lambda b, i: (b, 0, COL_IKB)),
                  pl.BlockSpec(bias.shape, lambda b, i: (0, 0, 0, 0))],
        out_specs=pl.BlockSpec((1, tb, GW), lambda b, i: (b, i, 0)),
        out_shape=jax.ShapeDtypeStruct((nb, seq, GW), BF16),
        scratch_shapes=[pltpu.VMEM((nkb, tb, tb), I32),
                        pltpu.VMEM((nkb, tb, tb), I16),
                        pltpu.VMEM((nkb, GW, tb), BF16),
                        pltpu.VMEM((tb, tb), F32),
                        pltpu.VMEM((ATT_HEADS, tb, tb), F32),
                        pltpu.VMEM((ATT_HEADS, tb, tb), BF16),
                        pltpu.VMEM((ATT_HEADS, 1, tb), F32),
                        pltpu.VMEM((ATT_HEADS, HEAD_DIM + PACK16, tb), F32)],
        compiler_params=pltpu.CompilerParams(
            dimension_semantics=("arbitrary", "arbitrary"), vmem_limit_bytes=VMEM_LIMIT),
        name="dsa_group",
    )(proj3, proj3, proj3, proj3, proj3, proj3, proj3, proj3, bias)


def _mem_kernel(q_ref, g_ref, kv_ref, o_ref):
    nt = (((1,), (1,)), ((), ()))
    scale = HEAD_DIM ** -0.5
    outs = []
    for h in range(MEM_HEADS):
        c0 = HEAD_DIM * h
        qh = q_ref[0, :, c0:c0 + HEAD_DIM]
        kh = kv_ref[0, :, c0:c0 + HEAD_DIM]
        vh = kv_ref[0, :, GW + c0:GW + c0 + HEAD_DIM]
        s = lax.dot_general(qh, kh, nt, preferred_element_type=F32) * scale
        m = jnp.max(s, axis=1, keepdims=True)
        p = jnp.exp(s - m)
        l = jnp.sum(p, axis=1, keepdims=True)
        outs.append(jnp.dot(p.astype(BF16), vh, preferred_element_type=F32) / l)
    y = jnp.concatenate(outs, axis=1)
    o_ref[0] = (y * _silu(g_ref[0].astype(F32))).astype(o_ref.dtype)


def _mem_call(proj3, mkv, tm):
    nb, seq, _ = proj3.shape
    n_mem = mkv.shape[1]
    return pl.pallas_call(
        _mem_kernel,
        grid=(nb, seq // tm),
        in_specs=[pl.BlockSpec((1, tm, GW), lambda b, s: (b, s, COL_MQ)),
                  pl.BlockSpec((1, tm, GW), lambda b, s: (b, s, COL_GD)),
                  pl.BlockSpec((1, n_mem, 2 * GW), lambda b, s: (b, 0, 0))],
        out_specs=pl.BlockSpec((1, tm, GW), lambda b, s: (b, s, 0)),
        out_shape=jax.ShapeDtypeStruct((nb, seq, GW), BF16),
        compiler_params=pltpu.CompilerParams(
            dimension_semantics=("parallel", "parallel"), vmem_limit_bytes=VMEM_LIMIT),
        name="mem_group",
    )(proj3, proj3, mkv)


def _out_kernel(ya_ref, yb_ref, yc_ref, yd_ref, x_ref, w_ref, g_ref, b_ref, o_ref, *, alpha):
    acc = alpha * x_ref[...]
    for n, y_ref in enumerate((ya_ref, yb_ref, yc_ref, yd_ref)):
        acc = acc + jnp.dot(y_ref[...], w_ref[GW * n:GW * (n + 1), :], preferred_element_type=F32)
    mu = jnp.mean(acc, axis=-1, keepdims=True)
    xc = acc - mu
    var = jnp.mean(xc * xc, axis=-1, keepdims=True)
    o_ref[...] = xc * lax.rsqrt(var + LN_EPS) * g_ref[...] + b_ref[...]


def _out_call(ys, x2, w_out, ln_g, ln_b, alpha, tm):
    t = x2.shape[0]
    kern = functools.partial(_out_kernel, alpha=alpha)
    row = lambda width: pl.BlockSpec((tm, width), lambda r: (r, 0))
    const = lambda *shape: pl.BlockSpec(shape, lambda r: (0,) * len(shape))
    return pl.pallas_call(
        kern,
        grid=(t // tm,),
        in_specs=[row(GW), row(GW), row(GW), row(GW), row(D_MODEL),
                  const(4 * GW, D_MODEL), const(1, D_MODEL), const(1, D_MODEL)],
        out_specs=row(D_MODEL),
        out_shape=jax.ShapeDtypeStruct((t, D_MODEL), F32),
        compiler_params=pltpu.CompilerParams(
            dimension_semantics=("parallel",), vmem_limit_bytes=VMEM_LIMIT),
        name="out_deepnorm",
    )(*ys, x2, w_out, ln_g, ln_b)


def _pack_w_in(w):
    d = w.shape[0]
    ik0 = 9 * GW + IDX_HEADS * IDX_DIM
    iw0 = ik0 + IDX_DIM
    md0 = iw0 + IDX_HEADS
    w_ik = w[:, ik0:iw0]
    zeros = lambda n: jnp.zeros((d, n), w.dtype)
    packed = jnp.concatenate(
        [w[:, :ik0], w[:, md0:md0 + 2 * GW],
         w_ik, zeros(LANES - IDX_DIM),
         zeros(LANES - IDX_DIM), w_ik,
         w[:, iw0:md0], zeros(LANES - IDX_HEADS),
         zeros(PACKED_WIDTH - 12 * GW - 3 * LANES)], axis=1)
    return packed.astype(BF16)


def _pick(n, pref):
    while n % pref:
        pref //= 2
    return pref


def kernel(x, mem, w_in, ssm_lam_re, ssm_lam_im, ssm_log_dt, ssm_b_re, ssm_b_im, ssm_c_re, ssm_c_im,
           ssm_d, ssm_w_glu, conv_w, conv_b, conv_ln_g, conv_ln_b, conv_w_pw, rel_bias, mem_w_kv,
           w_out, ln_g, ln_b):
    nb, seq, d = x.shape
    depth = w_in.shape[0]
    t = nb * seq
    alpha = (2 * depth) ** 0.25
    topk = min(TOPK_MAX, seq // 4)
    tb = 256
    tm = _pick(t, 1024)
    n_mem = mem.shape[1]

    bias = _bias_tiles(rel_bias, tb)
    mem2 = mem.reshape(nb * n_mem, d)
    x2 = x.reshape(t, d)
    for l in range(depth):
        proj = _matmul(x2, _pack_w_in(w_in[l]), tm, GW, "in_proj")
        proj3 = proj.reshape(nb, seq, PACKED_WIDTH)
        mkv = _matmul(mem2, mem_w_kv[l].astype(BF16), _pick(nb * n_mem, 1024), GW, "mem_kv")
        mkv = mkv.reshape(nb, n_mem, 2 * GW)

        bt, cre, cim, abar = _ssm_params(ssm_lam_re[l], ssm_lam_im[l], ssm_log_dt[l], ssm_b_re[l],
                                         ssm_b_im[l], ssm_c_re[l], ssm_c_im[l])
        y_a = _ssm_call(proj3, bt, cre, cim, abar, ssm_d[l].reshape(1, GW).astype(F32),
                        ssm_w_glu[l].astype(BF16), _pick(seq, 128))
        cw = jnp.concatenate([conv_w[l].reshape(CONV_WIDTH, GW).astype(F32),
                              jnp.zeros((CONV_HALO - CONV_WIDTH, GW), F32)], axis=0)
        y_b = _conv_call(proj3, cw, conv_b[l].reshape(1, GW).astype(F32),
                         conv_ln_g[l].reshape(1, GW).astype(F32), conv_ln_b[l].reshape(1, GW).astype(F32),
                         conv_w_pw[l].astype(BF16), _pick(seq, 512))
        y_c = _dsa_call(proj3, bias, tb, topk)
        y_d = _mem_call(proj3, mkv, _pick(seq, 512))

        ys = [y.reshape(t, GW) for y in (y_a, y_b, y_c, y_d)]
        x2 = _out_call(ys, x2, w_out[l].astype(BF16), ln_g[l].reshape(1, d).astype(F32),
                       ln_b[l].reshape(1, d).astype(F32), alpha, _pick(t, 512))
    return x2.reshape(nb, seq, d)
```

```python
import functools
import math

import jax
import jax.numpy as jnp
import numpy as np
from jax import lax
from jax.experimental import pallas as pl
from jax.experimental.pallas import tpu as pltpu

F32 = jnp.float32
BF16 = jnp.bfloat16
I32 = jnp.int32
I16 = jnp.int16

D_MODEL = 2048
GW = 512
SSM_CH = 16
SSM_GROUPS = GW // SSM_CH
SSM_STATE = 64
SSM_COMPLEX = SSM_GROUPS * SSM_STATE
CONV_WIDTH = 31
CONV_HALO = 32
ATT_HEADS = 4
HEAD_DIM = GW // ATT_HEADS
IDX_HEADS = 8
IDX_DIM = 64
TOPK_MAX = 256
REL_BUCKETS = 32
REL_MAX_DIST = 128
MEM_HEADS = 4
LN_EPS = 1e-5
NEG_BIG = -1e30
LOG2_E = math.log2(math.e)

LANES = 128
SUBLANES = 8
PACK16 = 16
COL_U, COL_GA, COL_VAL, COL_GLU, COL_GB, COL_Q, COL_K, COL_V, COL_GC, COL_IQ, COL_MQ, COL_GD = range(12)
COL_IKA = 12 * (GW // LANES)
COL_IKB = COL_IKA + 1
COL_IW = COL_IKA + 2
PACKED_WIDTH = 13 * GW

KEY_NEG_INF = int(np.array(0xFF800000, np.uint32).astype(np.int64) ^ 0x7FFFFFFF) - (1 << 32)
HALF_OFFSET = 1 << 15

VMEM_LIMIT = 56 * 1024 * 1024


def _sigmoid(x):
    return 1.0 / (1.0 + jnp.exp(-x))


def _silu(x):
    return x * _sigmoid(x)


def _matmul_kernel(x_ref, w_ref, o_ref, xb_ref):
    @pl.when(pl.program_id(1) == 0)
    def _():
        xb_ref[...] = x_ref[...].astype(BF16)

    o_ref[...] = jnp.dot(xb_ref[...], w_ref[...], preferred_element_type=F32).astype(o_ref.dtype)


def _matmul(x, w, tm, tn, name):
    m, k = x.shape
    n = w.shape[1]
    return pl.pallas_call(
        _matmul_kernel,
        grid=(m // tm, n // tn),
        in_specs=[pl.BlockSpec((tm, k), lambda i, j: (i, 0)),
                  pl.BlockSpec((k, tn), lambda i, j: (0, j))],
        out_specs=pl.BlockSpec((tm, tn), lambda i, j: (i, j)),
        out_shape=jax.ShapeDtypeStruct((m, n), BF16),
        scratch_shapes=[pltpu.VMEM((tm, k), BF16)],
        compiler_params=pltpu.CompilerParams(
            dimension_semantics=("parallel", "arbitrary"), vmem_limit_bytes=VMEM_LIMIT),
        name=name,
    )(x, w)


def _in_proj_kernel(x_ref, w_ref, o_ref, *, tn):
    xb = x_ref[...].astype(BF16)
    for n0 in range(0, w_ref.shape[1], tn):
        o_ref[:, n0:n0 + tn] = jnp.dot(xb, w_ref[:, n0:n0 + tn], preferred_element_type=F32).astype(o_ref.dtype)


def _in_proj(x, w, tm, tn):
    m, k = x.shape
    n = w.shape[1]
    return pl.pallas_call(
        functools.partial(_in_proj_kernel, tn=tn),
        grid=(m // tm,),
        in_specs=[pl.BlockSpec((tm, k), lambda i: (i, 0)),
                  pl.BlockSpec((k, n), lambda i: (0, 0), pipeline_mode=pl.Buffered(1))],
        out_specs=pl.BlockSpec((tm, n), lambda i: (i, 0)),
        out_shape=jax.ShapeDtypeStruct((m, n), BF16),
        compiler_params=pltpu.CompilerParams(
            dimension_semantics=("parallel",), vmem_limit_bytes=VMEM_LIMIT),
        name="in_proj",
    )(x, w)


def _ssm_kernel(u_ref, g_ref, bt_ref, cre_ref, cim_ref, abar_ref, d_ref, wglu_ref, o_ref,
                st_ref, bu_ref, *, nb, lc):
    rows = nb * lc
    n_re = SSM_COMPLEX

    @pl.when(pl.program_id(0) == 0)
    def _():
        st_ref[...] = jnp.zeros_like(st_ref)

    n_ct = n_re // LANES
    u = pltpu.einshape("btc->tbc", u_ref[...]).reshape(rows, GW)
    for jt in range(2 * n_re // 256):
        k0 = LANES * ((jt % (n_re // 256)) // 2)
        res = jnp.dot(u[:, k0:k0 + LANES], bt_ref[jt], preferred_element_type=F32)
        bu_ref[2 * jt] = res[:, :LANES]
        bu_ref[2 * jt + 1] = res[:, LANES:]

    slab = 4
    for ch in range(n_ct // slab):
        tiles = tuple(range(slab * ch, slab * (ch + 1)))
        a_re = [jnp.broadcast_to(abar_ref[:, LANES * c:LANES * (c + 1)], (nb, LANES)) for c in tiles]
        a_im = [jnp.broadcast_to(abar_ref[:, n_re + LANES * c:n_re + LANES * (c + 1)], (nb, LANES))
                for c in tiles]

        def step(t, carry, tiles=tiles, a_re=a_re, a_im=a_im):
            r = pl.ds(pl.multiple_of(t * nb, nb), nb)
            new = []
            for n, c in enumerate(tiles):
                h_re, h_im = carry[n]
                n_hre = a_re[n] * h_re - a_im[n] * h_im + bu_ref[c, r, :]
                n_him = a_re[n] * h_im + a_im[n] * h_re + bu_ref[n_ct + c, r, :]
                bu_ref[c, r, :] = n_hre
                bu_ref[n_ct + c, r, :] = n_him
                new.append((n_hre, n_him))
            return tuple(new)

        init = tuple((st_ref[c], st_ref[n_ct + c]) for c in tiles)
        final = lax.fori_loop(0, lc, step, init)
        for n, c in enumerate(tiles):
            st_ref[c] = final[n][0]
            st_ref[n_ct + c] = final[n][1]

    ys = []
    for m in range(GW // LANES):
        s_re = jnp.concatenate([bu_ref[4 * m + n] for n in range(4)], axis=1).astype(BF16)
        s_im = jnp.concatenate([bu_ref[n_ct + 4 * m + n] for n in range(4)], axis=1).astype(BF16)
        ys.append(jnp.dot(s_re, cre_ref[m], preferred_element_type=F32)
                  + jnp.dot(s_im, cim_ref[m], preferred_element_type=F32))
    y = jnp.concatenate(ys, axis=1) + d_ref[...] * u.astype(F32)
    z = jnp.dot(y.astype(BF16), wglu_ref[...], preferred_element_type=F32)
    glu = pltpu.einshape("tbc->btc", (z[:, :GW] * _sigmoid(z[:, GW:])).reshape(lc, nb, GW))
    o_ref[...] = (glu * _silu(g_ref[...].astype(F32))).astype(o_ref.dtype)


def _ssm_params(lam_re, lam_im, log_dt, b_re, b_im, c_re, c_im):
    lr = lam_re.astype(F32)
    li = lam_im.astype(F32)
    dt = jnp.exp(log_dt.astype(F32))[:, None]
    mag = jnp.exp(lr * dt)
    abar_re = mag * jnp.cos(li * dt)
    abar_im = mag * jnp.sin(li * dt)
    den = lr * lr + li * li
    nr = abar_re - 1.0
    f_re = (nr * lr + abar_im * li) / den
    f_im = (abar_im * lr - nr * li) / den
    br = b_re.astype(F32)
    bi = b_im.astype(F32)
    bbar_re = f_re[..., None] * br - f_im[..., None] * bi
    bbar_im = f_re[..., None] * bi + f_im[..., None] * br
    eye = jnp.eye(SSM_GROUPS, dtype=F32)
    bfull = jnp.concatenate(
        [jnp.einsum('gnc,gh->gchn', b, eye).reshape(GW, SSM_COMPLEX) for b in (bbar_re, bbar_im)], axis=1)
    n_tiles = SSM_COMPLEX // 256
    bt = jnp.stack([
        bfull[LANES * ((jt % n_tiles) // 2):LANES * ((jt % n_tiles) // 2 + 1), 256 * jt:256 * (jt + 1)]
        for jt in range(2 * n_tiles)]).astype(BF16)
    cfull_re = jnp.einsum('gcn,gh->gnhc', c_re.astype(F32), eye).reshape(SSM_COMPLEX, GW)
    cfull_im = -jnp.einsum('gcn,gh->gnhc', c_im.astype(F32), eye).reshape(SSM_COMPLEX, GW)
    cre = jnp.stack([cfull_re[512 * m:512 * (m + 1), LANES * m:LANES * (m + 1)]
                     for m in range(GW // LANES)]).astype(BF16)
    cim = jnp.stack([cfull_im[512 * m:512 * (m + 1), LANES * m:LANES * (m + 1)]
                     for m in range(GW // LANES)]).astype(BF16)
    abar = jnp.concatenate([abar_re.reshape(1, SSM_COMPLEX), abar_im.reshape(1, SSM_COMPLEX)], axis=1)
    return bt, cre, cim, abar


def _ssm_call(proj3, bt, cre, cim, abar, d_skip, w_glu, lc):
    nb, seq, _ = proj3.shape
    kern = functools.partial(_ssm_kernel, nb=nb, lc=lc)
    const = lambda *shape: pl.BlockSpec(shape, lambda c: (0,) * len(shape))
    return pl.pallas_call(
        kern,
        grid=(seq // lc,),
        in_specs=[pl.BlockSpec((nb, lc, GW), lambda c: (0, c, COL_U)),
                  pl.BlockSpec((nb, lc, GW), lambda c: (0, c, COL_GA)),
                  const(*bt.shape), const(*cre.shape), const(*cim.shape), const(*abar.shape),
                  const(1, GW), const(GW, 2 * GW)],
        out_specs=pl.BlockSpec((nb, lc, GW), lambda c: (0, c, 0)),
        out_shape=jax.ShapeDtypeStruct((nb, seq, GW), BF16),
        scratch_shapes=[pltpu.VMEM((2 * SSM_COMPLEX // LANES, nb, LANES), F32),
                        pltpu.VMEM((2 * SSM_COMPLEX // LANES, nb * lc, LANES), F32)],
        compiler_params=pltpu.CompilerParams(
            dimension_semantics=("arbitrary",), vmem_limit_bytes=VMEM_LIMIT),
        name="ssm_group",
    )(proj3, proj3, bt, cre, cim, abar, d_skip, w_glu)


def _conv_kernel(val_ref, glu_ref, g_ref, cw_ref, cb_ref, lg_ref, lb_ref, wpw_ref, o_ref, h_ref, sh_ref, *, tc):
    @pl.when(pl.program_id(1) == 0)
    def _():
        h_ref[0:CONV_HALO, :] = jnp.zeros((CONV_HALO, GW), F32)

    @pl.when(pl.program_id(1) != 0)
    def _():
        h_ref[0:CONV_HALO, :] = h_ref[tc:tc + CONV_HALO, :]

    val = val_ref[0].astype(F32)
    h_ref[CONV_HALO:CONV_HALO + tc, :] = val * _sigmoid(glu_ref[0].astype(F32))

    span = tc + CONV_HALO - SUBLANES
    for r in range(1, SUBLANES):
        sh_ref[r - 1] = h_ref[r:r + span, :]

    rc = 64
    first = CONV_HALO - (CONV_WIDTH - 1)
    for r0 in range(0, tc, rc):
        acc = jnp.zeros((rc, GW), F32)
        for k in range(CONV_WIDTH):
            a, r = divmod(first + k, SUBLANES)
            lo = r0 + SUBLANES * a
            src = h_ref[lo:lo + rc, :] if r == 0 else sh_ref[r - 1, lo:lo + rc, :]
            acc = acc + cw_ref[k:k + 1, :] * src
        hc = acc + cb_ref[...]
        mu = jnp.mean(hc, axis=-1, keepdims=True)
        xc = hc - mu
        var = jnp.mean(xc * xc, axis=-1, keepdims=True)
        hn = xc * lax.rsqrt(var + LN_EPS) * lg_ref[...] + lb_ref[...]
        hs = _silu(hn)
        y = jnp.dot(hs.astype(BF16), wpw_ref[...], preferred_element_type=F32)
        o_ref[0, r0:r0 + rc, :] = (y * _silu(g_ref[0, r0:r0 + rc, :].astype(F32))).astype(o_ref.dtype)


def _conv_call(proj3, cw, cb, lg, lb, wpw, tc):
    nb, seq, _ = proj3.shape
    kern = functools.partial(_conv_kernel, tc=tc)
    const = lambda *shape: pl.BlockSpec(shape, lambda b, s: (0,) * len(shape))
    return pl.pallas_call(
        kern,
        grid=(nb, seq // tc),
        in_specs=[pl.BlockSpec((1, tc, GW), lambda b, s: (b, s, COL_VAL)),
                  pl.BlockSpec((1, tc, GW), lambda b, s: (b, s, COL_GLU)),
                  pl.BlockSpec((1, tc, GW), lambda b, s: (b, s, COL_GB)),
                  const(CONV_HALO, GW), const(1, GW), const(1, GW), const(1, GW), const(GW, GW)],
        out_specs=pl.BlockSpec((1, tc, GW), lambda b, s: (b, s, 0)),
        out_shape=jax.ShapeDtypeStruct((nb, seq, GW), BF16),
        scratch_shapes=[pltpu.VMEM((tc + CONV_HALO, GW), F32),
                        pltpu.VMEM((SUBLANES - 1, tc + CONV_HALO - SUBLANES, GW), F32)],
        compiler_params=pltpu.CompilerParams(
            dimension_semantics=("parallel", "arbitrary"), vmem_limit_bytes=VMEM_LIMIT),
        name="conv_group",
    )(proj3, proj3, proj3, cw, cb, lg, lb, wpw)


def _dsa_kernel(q_ref, g_ref, iq_ref, iw_ref, k_ref, v_ref, ika_ref, ikb_ref, bias_ref, o_ref,
                key_scr, half_scr, vt_scr, mask_scr, s0_scr, s1_scr, p_scr, m_scr, alpha_scr, acc_scr,
                *, tb, topk, seq):
    i = pl.program_id(1)
    nkb = i + 1
    nt = (((1,), (1,)), ((), ()))
    idx_bits = max(1, (seq - 1).bit_length())
    row_idx = lax.broadcasted_iota(I32, (tb, tb), 0)
    lane_idx = lax.broadcasted_iota(I32, (tb, tb), 1)

    @pl.when(i == 0)
    def _():
        for j in range(seq // tb):
            vt_scr[j] = v_ref[0, j * tb:(j + 1) * tb, :].T

    iw_t = iw_ref[0].astype(F32).T[0:IDX_HEADS, :]

    kh_rows = tb // 2
    part_row = lax.broadcasted_iota(I32, (kh_rows, tb), 0)
    part_lane = lax.broadcasted_iota(I32, (kh_rows, tb), 1)

    def score_body(j, carry):
        for part in range(tb // kh_rows):
            off = pl.multiple_of(j * tb + part * kh_rows, kh_rows)
            ka = ika_ref[0, pl.ds(off, kh_rows), :]
            kb = ikb_ref[0, pl.ds(off, kh_rows), :]
            sc = jnp.zeros((kh_rows, tb), F32)
            for p in range(IDX_HEADS // 2):
                iqp = iq_ref[0, :, LANES * p:LANES * (p + 1)]
                da = lax.dot_general(ka, iqp, nt, preferred_element_type=F32)
                db = lax.dot_general(kb, iqp, nt, preferred_element_type=F32)
                sc = (sc + iw_t[2 * p:2 * p + 1, :] * jnp.maximum(da, 0.0)
                      + iw_t[2 * p + 1:2 * p + 2, :] * jnp.maximum(db, 0.0))
            sc = jnp.where(off + part_row <= i * tb + part_lane, sc, -jnp.inf)
            bits = pltpu.bitcast(sc, I32)
            key = bits ^ ((bits >> 31) & 0x7FFFFFFF)
            key = jnp.where(key == -1, 0, key)
            key_scr[j, part * kh_rows:(part + 1) * kh_rows, :] = key
            half_scr[j, part * kh_rows:(part + 1) * kh_rows, :] = (key >> 16).astype(I16)
        return carry

    lax.fori_loop(0, nkb, score_body, 0)

    one16 = jnp.ones((), BF16)
    zero16 = jnp.zeros((), BF16)

    def count16(cand):
        c16 = jnp.broadcast_to(cand, (PACK16, tb)).astype(I16)

        def body(j, accs):
            a0, a1 = accs
            kt = half_scr[j]
            for r in range(0, tb // PACK16, 2):
                a0 = a0 + jnp.where(kt[PACK16 * r:PACK16 * (r + 1), :] >= c16, one16, zero16)
                a1 = a1 + jnp.where(kt[PACK16 * (r + 1):PACK16 * (r + 2), :] >= c16, one16, zero16)
            return a0, a1

        zeros = jnp.zeros((PACK16, tb), BF16)
        a0, a1 = lax.fori_loop(0, nkb, body, (zeros, zeros))
        return jnp.sum(a0.astype(F32) + a1.astype(F32), axis=0, keepdims=True)

    def search(target, nbits, offset):
        def body(it, state):
            best, above = state
            cand = best | lax.shift_left(jnp.int32(1), nbits - 1 - it)
            cnt = count16(cand - offset)
            ok = cnt >= target
            return jnp.where(ok, cand, best), jnp.where(ok, above, cnt)

        best, above = lax.fori_loop(0, nbits, body, (jnp.zeros((1, tb), I32), jnp.zeros((1, tb), F32)))
        return best - offset, above

    tau_hi, above = search(topk, 16, HALF_OFFSET)
    need = topk - above

    def low_body(j, carry):
        kt = key_scr[j]
        half_scr[j] = jnp.where((kt >> 16) == tau_hi, (kt & 0xFFFF) - HALF_OFFSET, -HALF_OFFSET).astype(I16)
        return carry

    lax.fori_loop(0, nkb, low_body, 0)
    tau_lo, above = search(need, 16, HALF_OFFSET)
    need = need - above
    tau = tau_hi * 65536 + (tau_lo + HALF_OFFSET)

    def tie_body(j, carry):
        half_scr[j] = jnp.where(key_scr[j] == tau, (seq - 1 - j * tb) - row_idx, -1).astype(I16)
        return carry

    lax.fori_loop(0, nkb, tie_body, 0)
    rank, _ = search(need, idx_bits, 0)
    few = tau <= KEY_NEG_INF
    tau = jnp.where(few, KEY_NEG_INF + 1, tau)
    last_tie = jnp.where(few, seq, seq - 1 - rank)

    def demote_body(j, carry):
        kt = key_scr[j]
        drop = (kt == tau) & (j * tb + row_idx > last_tie)
        key_scr[j] = jnp.where(drop, kt - 1, kt)
        return carry

    lax.fori_loop(0, nkb, demote_body, 0)

    m_scr[...] = jnp.full(m_scr.shape, NEG_BIG, F32)
    acc_scr[...] = jnp.zeros(acc_scr.shape, F32)
    c_log2 = HEAD_DIM ** -0.5 * LOG2_E
    ones_rows = jnp.ones((PACK16, tb), BF16)

    def logits_stage(j, s_scr):
        off = pl.multiple_of(j * tb, tb)
        mask_scr[...] = jnp.where(key_scr[j] >= tau, 0.0, NEG_BIG)
        bsel = jnp.minimum(i - j, 2)
        for h in range(ATT_HEADS):
            c0 = HEAD_DIM * h
            kh = k_ref[0, pl.ds(off, tb), c0:c0 + HEAD_DIM]
            qh = q_ref[0, :, c0:c0 + HEAD_DIM]
            s_scr[h] = (lax.dot_general(kh, qh, nt, preferred_element_type=F32) * c_log2
                        + bias_ref[bsel, h] + mask_scr[...])

    def softmax_stage(s_scr):
        m_new = []
        for h in range(ATT_HEADS):
            m_prev = m_scr[h]
            m_new.append(jnp.maximum(m_prev, jnp.max(s_scr[h], axis=0, keepdims=True)))
            alpha_scr[h] = jnp.exp2(m_prev - m_new[h])
            m_scr[h] = m_new[h]
        for h in range(ATT_HEADS):
            p_scr[h] = jnp.exp2(s_scr[h] - m_new[h]).astype(BF16)

    def value_stage(j):
        for h in range(ATT_HEADS):
            c0 = HEAD_DIM * h
            vt = jnp.concatenate([vt_scr[j, c0:c0 + HEAD_DIM, :], ones_rows], axis=0)
            acc_scr[h] = alpha_scr[h] * acc_scr[h] + jnp.dot(vt, p_scr[h], preferred_element_type=F32)

    def att_step(j, s_cur, s_next):
        value_stage(jnp.maximum(j - 1, 0))
        logits_stage(jnp.minimum(j + 1, nkb - 1), s_next)
        softmax_stage(s_cur)

    def att_body(j, carry):
        @pl.when(j % 2 == 0)
        def _():
            att_step(j, s0_scr, s1_scr)

        @pl.when(j % 2 == 1)
        def _():
            att_step(j, s1_scr, s0_scr)

        return carry

    p_scr[...] = jnp.zeros(p_scr.shape, BF16)
    alpha_scr[...] = jnp.ones(alpha_scr.shape, F32)
    logits_stage(0, s0_scr)
    lax.fori_loop(0, nkb, att_body, 0)
    value_stage(nkb - 1)

    y = jnp.concatenate(
        [(acc_scr[h, 0:HEAD_DIM, :] / acc_scr[h, HEAD_DIM:HEAD_DIM + 1, :]).T for h in range(ATT_HEADS)], axis=1)
    o_ref[0] = (y * _silu(g_ref[0].astype(F32))).astype(o_ref.dtype)


def _rel_bucket(dist):
    n = jnp.maximum(dist, 0)
    max_exact = REL_BUCKETS // 2
    nf = jnp.maximum(n, 1).astype(F32)
    large = max_exact + (jnp.log(nf / max_exact) / math.log(REL_MAX_DIST / max_exact)
                         * (REL_BUCKETS - max_exact)).astype(I32)
    large = jnp.minimum(large, REL_BUCKETS - 1)
    return jnp.where(n < max_exact, n, large)


def _bias_tiles(rel_bias, tb):
    assert tb >= REL_MAX_DIST
    s = jnp.arange(tb, dtype=I32)[:, None]
    q = jnp.arange(tb, dtype=I32)[None, :]
    bucket = jnp.stack([_rel_bucket(delta + q - s) for delta in (0, tb, 2 * tb)])[:, None]
    table = rel_bias.astype(F32) * LOG2_E
    tiles = jnp.zeros((3, ATT_HEADS, tb, tb), F32)
    for b in range(REL_BUCKETS):
        tiles = jnp.where(bucket == b, table[b][None, :, None, None], tiles)
    return tiles


def _dsa_call(proj3, bias, tb, topk):
    nb, seq, _ = proj3.shape
    nkb = seq // tb
    kern = functools.partial(_dsa_kernel, tb=tb, topk=float(topk), seq=seq)
    qblk = lambda col: pl.BlockSpec((1, tb, GW), lambda b, i: (b, i, col))
    return pl.pallas_call(
        kern,
        grid=(nb, nkb),
        in_specs=[qblk(COL_Q), qblk(COL_GC), qblk(COL_IQ),
                  pl.BlockSpec((1, tb, LANES), lambda b, i: (b, i, COL_IW)),
                  pl.BlockSpec((1, seq, GW), lambda b, i: (b, 0, COL_K)),
                  pl.BlockSpec((1, seq, GW), lambda b, i: (b, 0, COL_V)),
                  pl.BlockSpec((1, seq, LANES), lambda b, i: (b, 0, COL_IKA)),
                  pl.BlockSpec((1, seq, LANES), lambda b, i: (b, 0, COL_IKB)),
                  pl.BlockSpec(bias.shape, lambda b, i: (0, 0, 0, 0))],
        out_specs=pl.BlockSpec((1, tb, GW), lambda b, i: (b, i, 0)),
        out_shape=jax.ShapeDtypeStruct((nb, seq, GW), BF16),
        scratch_shapes=[pltpu.VMEM((nkb, tb, tb), I32),
                        pltpu.VMEM((nkb, tb, tb), I16),
                        pltpu.VMEM((nkb, GW, tb), BF16),
                        pltpu.VMEM((tb, tb), F32),
                        pltpu.VMEM((ATT_HEADS, tb, tb), F32),
                        pltpu.VMEM((ATT_HEADS, tb, tb), F32),
                        pltpu.VMEM((ATT_HEADS, tb, tb), BF16),
                        pltpu.VMEM((ATT_HEADS, 1, tb), F32),
                        pltpu.VMEM((ATT_HEADS, 1, tb), F32),
                        pltpu.VMEM((ATT_HEADS, HEAD_DIM + PACK16, tb), F32)],
        compiler_params=pltpu.CompilerParams(
            dimension_semantics=("arbitrary", "arbitrary"), vmem_limit_bytes=VMEM_LIMIT),
        name="dsa_group",
    )(proj3, proj3, proj3, proj3, proj3, proj3, proj3, proj3, bias)


def _mem_kernel(q_ref, g_ref, kv_ref, o_ref):
    nt = (((1,), (1,)), ((), ()))
    scale = HEAD_DIM ** -0.5
    outs = []
    for h in range(MEM_HEADS):
        c0 = HEAD_DIM * h
        qh = q_ref[0, :, c0:c0 + HEAD_DIM]
        kh = kv_ref[0, :, c0:c0 + HEAD_DIM]
        vh = kv_ref[0, :, GW + c0:GW + c0 + HEAD_DIM]
        s = lax.dot_general(qh, kh, nt, preferred_element_type=F32) * scale
        m = jnp.max(s, axis=1, keepdims=True)
        p = jnp.exp(s - m)
        l = jnp.sum(p, axis=1, keepdims=True)
        outs.append(jnp.dot(p.astype(BF16), vh, preferred_element_type=F32) / l)
    y = jnp.concatenate(outs, axis=1)
    o_ref[0] = (y * _silu(g_ref[0].astype(F32))).astype(o_ref.dtype)


def _mem_call(proj3, mkv, tm):
    nb, seq, _ = proj3.shape
    n_mem = mkv.shape[1]
    return pl.pallas_call(
        _mem_kernel,
        grid=(nb, seq // tm),
        in_specs=[pl.BlockSpec((1, tm, GW), lambda b, s: (b, s, COL_MQ)),
                  pl.BlockSpec((1, tm, GW), lambda b, s: (b, s, COL_GD)),
                  pl.BlockSpec((1, n_mem, 2 * GW), lambda b, s: (b, 0, 0))],
        out_specs=pl.BlockSpec((1, tm, GW), lambda b, s: (b, s, 0)),
        out_shape=jax.ShapeDtypeStruct((nb, seq, GW), BF16),
        compiler_params=pltpu.CompilerParams(
            dimension_semantics=("parallel", "parallel"), vmem_limit_bytes=VMEM_LIMIT),
        name="mem_group",
    )(proj3, proj3, mkv)


def _out_kernel(ya_ref, yb_ref, yc_ref, yd_ref, x_ref, w_ref, g_ref, b_ref, o_ref, *, alpha):
    acc = alpha * x_ref[...]
    for n, y_ref in enumerate((ya_ref, yb_ref, yc_ref, yd_ref)):
        acc = acc + jnp.dot(y_ref[...], w_ref[GW * n:GW * (n + 1), :], preferred_element_type=F32)
    mu = jnp.mean(acc, axis=-1, keepdims=True)
    xc = acc - mu
    var = jnp.mean(xc * xc, axis=-1, keepdims=True)
    o_ref[...] = xc * lax.rsqrt(var + LN_EPS) * g_ref[...] + b_ref[...]


def _out_call(ys, x2, w_out, ln_g, ln_b, alpha, tm):
    t = x2.shape[0]
    kern = functools.partial(_out_kernel, alpha=alpha)
    row = lambda width: pl.BlockSpec((tm, width), lambda r: (r, 0))
    const = lambda *shape: pl.BlockSpec(shape, lambda r: (0,) * len(shape))
    return pl.pallas_call(
        kern,
        grid=(t // tm,),
        in_specs=[row(GW), row(GW), row(GW), row(GW), row(D_MODEL),
                  const(4 * GW, D_MODEL), const(1, D_MODEL), const(1, D_MODEL)],
        out_specs=row(D_MODEL),
        out_shape=jax.ShapeDtypeStruct((t, D_MODEL), F32),
        compiler_params=pltpu.CompilerParams(
            dimension_semantics=("parallel",), vmem_limit_bytes=VMEM_LIMIT),
        name="out_deepnorm",
    )(*ys, x2, w_out, ln_g, ln_b)


def _pack_w_in(w):
    d = w.shape[0]
    ik0 = 9 * GW + IDX_HEADS * IDX_DIM
    iw0 = ik0 + IDX_DIM
    md0 = iw0 + IDX_HEADS
    w_ik = w[:, ik0:iw0]
    zeros = lambda n: jnp.zeros((d, n), w.dtype)
    packed = jnp.concatenate(
        [w[:, :ik0], w[:, md0:md0 + 2 * GW],
         w_ik, zeros(LANES - IDX_DIM),
         zeros(LANES - IDX_DIM), w_ik,
         w[:, iw0:md0], zeros(LANES - IDX_HEADS),
         zeros(PACKED_WIDTH - 12 * GW - 3 * LANES)], axis=1)
    return packed.astype(BF16)


def _pick(n, pref):
    while n % pref:
        pref //= 2
    return pref


def kernel(x, mem, w_in, ssm_lam_re, ssm_lam_im, ssm_log_dt, ssm_b_re, ssm_b_im, ssm_c_re, ssm_c_im,
           ssm_d, ssm_w_glu, conv_w, conv_b, conv_ln_g, conv_ln_b, conv_w_pw, rel_bias, mem_w_kv,
           w_out, ln_g, ln_b):
    nb, seq, d = x.shape
    depth = w_in.shape[0]
    t = nb * seq
    alpha = (2 * depth) ** 0.25
    topk = min(TOPK_MAX, seq // 4)
    tb = 256
    tm = _pick(t, 1024)
    n_mem = mem.shape[1]

    bias = _bias_tiles(rel_bias, tb)
    mem2 = mem.reshape(nb * n_mem, d)
    x2 = x.reshape(t, d)
    for l in range(depth):
        proj = _in_proj(x2, _pack_w_in(w_in[l]), _pick(t, 512), GW)
        proj3 = proj.reshape(nb, seq, PACKED_WIDTH)
        mkv = _matmul(mem2, mem_w_kv[l].astype(BF16), _pick(nb * n_mem, 1024), GW, "mem_kv")
        mkv = mkv.reshape(nb, n_mem, 2 * GW)

        bt, cre, cim, abar = _ssm_params(ssm_lam_re[l], ssm_lam_im[l], ssm_log_dt[l], ssm_b_re[l],
                                         ssm_b_im[l], ssm_c_re[l], ssm_c_im[l])
        y_a = _ssm_call(proj3, bt, cre, cim, abar, ssm_d[l].reshape(1, GW).astype(F32),
                        ssm_w_glu[l].astype(BF16), _pick(seq, 128))
        cw = jnp.concatenate([conv_w[l].reshape(CONV_WIDTH, GW).astype(F32),
                              jnp.zeros((CONV_HALO - CONV_WIDTH, GW), F32)], axis=0)
        y_b = _conv_call(proj3, cw, conv_b[l].reshape(1, GW).astype(F32),
                         conv_ln_g[l].reshape(1, GW).astype(F32), conv_ln_b[l].reshape(1, GW).astype(F32),
                         conv_w_pw[l].astype(BF16), _pick(seq, 512))
        y_c = _dsa_call(proj3, bias, tb, topk)
        y_d = _mem_call(proj3, mkv, _pick(seq, 512))

        ys = [y.reshape(t, GW) for y in (y_a, y_b, y_c, y_d)]
        x2 = _out_call(ys, x2, w_out[l].astype(BF16), ln_g[l].reshape(1, d).astype(F32),
                       ln_b[l].reshape(1, d).astype(F32), alpha, _pick(t, 512))
    return x2.reshape(nb, seq, d)
```

```python
import functools
import math

import jax
import jax.numpy as jnp
import numpy as np
from jax import lax
from jax.experimental import pallas as pl
from jax.experimental.pallas import tpu as pltpu

F32 = jnp.float32
BF16 = jnp.bfloat16
I32 = jnp.int32
I16 = jnp.int16

D_MODEL = 2048
GW = 512
SSM_CH = 16
SSM_GROUPS = GW // SSM_CH
SSM_STATE = 64
SSM_COMPLEX = SSM_GROUPS * SSM_STATE
CONV_WIDTH = 31
CONV_HALO = 32
ATT_HEADS = 4
HEAD_DIM = GW // ATT_HEADS
IDX_HEADS = 8
IDX_DIM = 64
TOPK_MAX = 256
REL_BUCKETS = 32
REL_MAX_DIST = 128
MEM_HEADS = 4
LN_EPS = 1e-5
NEG_BIG = -1e30
LOG2_E = math.log2(math.e)

LANES = 128
SUBLANES = 8
PACK16 = 16
COL_U, COL_GA, COL_VAL, COL_GLU, COL_GB, COL_Q, COL_K, COL_V, COL_GC, COL_IQ, COL_MQ, COL_GD = range(12)
COL_IKA = 12 * (GW // LANES)
COL_IKB = COL_IKA + 1
COL_IW = COL_IKA + 2
PACKED_WIDTH = 13 * GW

KEY_NEG_INF = int(np.array(0xFF800000, np.uint32).astype(np.int64) ^ 0x7FFFFFFF) - (1 << 32)
HALF_OFFSET = 1 << 15

VMEM_LIMIT = 56 * 1024 * 1024


def _sigmoid(x):
    return 1.0 / (1.0 + jnp.exp(-x))


def _silu(x):
    return x * _sigmoid(x)


def _matmul_kernel(x_ref, w_ref, o_ref, xb_ref):
    @pl.when(pl.program_id(1) == 0)
    def _():
        xb_ref[...] = x_ref[...].astype(BF16)

    o_ref[...] = jnp.dot(xb_ref[...], w_ref[...], preferred_element_type=F32).astype(o_ref.dtype)


def _matmul(x, w, tm, tn, name):
    m, k = x.shape
    n = w.shape[1]
    return pl.pallas_call(
        _matmul_kernel,
        grid=(m // tm, n // tn),
        in_specs=[pl.BlockSpec((tm, k), lambda i, j: (i, 0)),
                  pl.BlockSpec((k, tn), lambda i, j: (0, j))],
        out_specs=pl.BlockSpec((tm, tn), lambda i, j: (i, j)),
        out_shape=jax.ShapeDtypeStruct((m, n), BF16),
        scratch_shapes=[pltpu.VMEM((tm, k), BF16)],
        compiler_params=pltpu.CompilerParams(
            dimension_semantics=("parallel", "arbitrary"), vmem_limit_bytes=VMEM_LIMIT),
        name=name,
    )(x, w)


def _in_proj_kernel(x_ref, w_ref, o_ref, *, tn):
    xb = x_ref[...].astype(BF16)
    for n0 in range(0, w_ref.shape[1], tn):
        o_ref[:, n0:n0 + tn] = jnp.dot(xb, w_ref[:, n0:n0 + tn], preferred_element_type=F32).astype(o_ref.dtype)


def _in_proj(x, w, tm, tn):
    m, k = x.shape
    n = w.shape[1]
    return pl.pallas_call(
        functools.partial(_in_proj_kernel, tn=tn),
        grid=(m // tm,),
        in_specs=[pl.BlockSpec((tm, k), lambda i: (i, 0)),
                  pl.BlockSpec((k, n), lambda i: (0, 0), pipeline_mode=pl.Buffered(1))],
        out_specs=pl.BlockSpec((tm, n), lambda i: (i, 0)),
        out_shape=jax.ShapeDtypeStruct((m, n), BF16),
        compiler_params=pltpu.CompilerParams(
            dimension_semantics=("parallel",), vmem_limit_bytes=VMEM_LIMIT),
        name="in_proj",
    )(x, w)


def _ssm_kernel(u_ref, g_ref, bt_ref, cre_ref, cim_ref, abar_ref, d_ref, wglu_ref, o_ref,
                st_ref, bu_ref, *, nb, lc):
    rows = nb * lc
    n_re = SSM_COMPLEX

    @pl.when(pl.program_id(0) == 0)
    def _():
        st_ref[...] = jnp.zeros_like(st_ref)

    n_ct = n_re // LANES
    u = pltpu.einshape("btc->tbc", u_ref[...]).reshape(rows, GW)
    for jt in range(2 * n_re // 256):
        k0 = LANES * ((jt % (n_re // 256)) // 2)
        res = jnp.dot(u[:, k0:k0 + LANES], bt_ref[jt], preferred_element_type=F32)
        bu_ref[2 * jt] = res[:, :LANES]
        bu_ref[2 * jt + 1] = res[:, LANES:]

    slab = 4
    for ch in range(n_ct // slab):
        tiles = tuple(range(slab * ch, slab * (ch + 1)))
        a_re = [jnp.broadcast_to(abar_ref[:, LANES * c:LANES * (c + 1)], (nb, LANES)) for c in tiles]
        a_im = [jnp.broadcast_to(abar_ref[:, n_re + LANES * c:n_re + LANES * (c + 1)], (nb, LANES))
                for c in tiles]

        def step(t, carry, tiles=tiles, a_re=a_re, a_im=a_im):
            r = pl.ds(pl.multiple_of(t * nb, nb), nb)
            new = []
            for n, c in enumerate(tiles):
                h_re, h_im = carry[n]
                n_hre = a_re[n] * h_re - a_im[n] * h_im + bu_ref[c, r, :]
                n_him = a_re[n] * h_im + a_im[n] * h_re + bu_ref[n_ct + c, r, :]
                bu_ref[c, r, :] = n_hre
                bu_ref[n_ct + c, r, :] = n_him
                new.append((n_hre, n_him))
            return tuple(new)

        init = tuple((st_ref[c], st_ref[n_ct + c]) for c in tiles)
        final = lax.fori_loop(0, lc, step, init, unroll=4)
        for n, c in enumerate(tiles):
            st_ref[c] = final[n][0]
            st_ref[n_ct + c] = final[n][1]

    ys = []
    for m in range(GW // LANES):
        s_re = jnp.concatenate([bu_ref[4 * m + n] for n in range(4)], axis=1).astype(BF16)
        s_im = jnp.concatenate([bu_ref[n_ct + 4 * m + n] for n in range(4)], axis=1).astype(BF16)
        ys.append(jnp.dot(s_re, cre_ref[m], preferred_element_type=F32)
                  + jnp.dot(s_im, cim_ref[m], preferred_element_type=F32))
    y = jnp.concatenate(ys, axis=1) + d_ref[...] * u.astype(F32)
    z = jnp.dot(y.astype(BF16), wglu_ref[...], preferred_element_type=F32)
    glu = pltpu.einshape("tbc->btc", (z[:, :GW] * _sigmoid(z[:, GW:])).reshape(lc, nb, GW))
    o_ref[...] = (glu * _silu(g_ref[...].astype(F32))).astype(o_ref.dtype)


def _ssm_params(lam_re, lam_im, log_dt, b_re, b_im, c_re, c_im):
    lr = lam_re.astype(F32)
    li = lam_im.astype(F32)
    dt = jnp.exp(log_dt.astype(F32))[:, None]
    mag = jnp.exp(lr * dt)
    abar_re = mag * jnp.cos(li * dt)
    abar_im = mag * jnp.sin(li * dt)
    den = lr * lr + li * li
    nr = abar_re - 1.0
    f_re = (nr * lr + abar_im * li) / den
    f_im = (abar_im * lr - nr * li) / den
    br = b_re.astype(F32)
    bi = b_im.astype(F32)
    bbar_re = f_re[..., None] * br - f_im[..., None] * bi
    bbar_im = f_re[..., None] * bi + f_im[..., None] * br
    n_tiles = SSM_COMPLEX // 256
    gpt = 256 // SSM_STATE
    gpr = LANES // SSM_CH
    meet = np.zeros((n_tiles, gpr, gpt), np.float32)
    for jj in range(n_tiles):
        for b in range(gpt):
            meet[jj, gpt * (jj % 2) + b, b] = 1.0
    bbar = jnp.stack([bbar_re, bbar_im]).reshape(2, n_tiles, gpt, SSM_STATE, SSM_CH)
    bt = jnp.einsum('pjbnc,jab->pjacbn', bbar, jnp.asarray(meet)).reshape(2 * n_tiles, LANES, 256).astype(BF16)
    eye = jnp.eye(gpr, dtype=F32)
    c_blk = lambda c: jnp.einsum('macn,ab->manbc', c.astype(F32).reshape(GW // LANES, gpr, SSM_CH, SSM_STATE),
                                 eye).reshape(GW // LANES, gpr * SSM_STATE, LANES).astype(BF16)
    cre = c_blk(c_re)
    cim = c_blk(-c_im)
    abar = jnp.concatenate([abar_re.reshape(1, SSM_COMPLEX), abar_im.reshape(1, SSM_COMPLEX)], axis=1)
    return bt, cre, cim, abar


def _ssm_call(proj3, bt, cre, cim, abar, d_skip, w_glu, lc):
    nb, seq, _ = proj3.shape
    kern = functools.partial(_ssm_kernel, nb=nb, lc=lc)
    const = lambda *shape: pl.BlockSpec(shape, lambda c: (0,) * len(shape))
    return pl.pallas_call(
        kern,
        grid=(seq // lc,),
        in_specs=[pl.BlockSpec((nb, lc, GW), lambda c: (0, c, COL_U)),
                  pl.BlockSpec((nb, lc, GW), lambda c: (0, c, COL_GA)),
                  const(*bt.shape), const(*cre.shape), const(*cim.shape), const(*abar.shape),
                  const(1, GW), const(GW, 2 * GW)],
        out_specs=pl.BlockSpec((nb, lc, GW), lambda c: (0, c, 0)),
        out_shape=jax.ShapeDtypeStruct((nb, seq, GW), BF16),
        scratch_shapes=[pltpu.VMEM((2 * SSM_COMPLEX // LANES, nb, LANES), F32),
                        pltpu.VMEM((2 * SSM_COMPLEX // LANES, nb * lc, LANES), F32)],
        compiler_params=pltpu.CompilerParams(
            dimension_semantics=("arbitrary",), vmem_limit_bytes=VMEM_LIMIT),
        name="ssm_group",
    )(proj3, proj3, bt, cre, cim, abar, d_skip, w_glu)


def _conv_kernel(val_ref, glu_ref, g_ref, cw_ref, cb_ref, lg_ref, lb_ref, wpw_ref, o_ref, h_ref, sh_ref, *, tc):
    @pl.when(pl.program_id(1) == 0)
    def _():
        h_ref[0:CONV_HALO, :] = jnp.zeros((CONV_HALO, GW), F32)

    @pl.when(pl.program_id(1) != 0)
    def _():
        h_ref[0:CONV_HALO, :] = h_ref[tc:tc + CONV_HALO, :]

    val = val_ref[0].astype(F32)
    h_ref[CONV_HALO:CONV_HALO + tc, :] = val * _sigmoid(glu_ref[0].astype(F32))

    span = tc + CONV_HALO - SUBLANES
    for r in range(1, SUBLANES):
        sh_ref[r - 1] = h_ref[r:r + span, :]

    rc = 64
    first = CONV_HALO - (CONV_WIDTH - 1)
    for r0 in range(0, tc, rc):
        acc = jnp.zeros((rc, GW), F32)
        for k in range(CONV_WIDTH):
            a, r = divmod(first + k, SUBLANES)
            lo = r0 + SUBLANES * a
            src = h_ref[lo:lo + rc, :] if r == 0 else sh_ref[r - 1, lo:lo + rc, :]
            acc = acc + cw_ref[k:k + 1, :] * src
        hc = acc + cb_ref[...]
        mu = jnp.mean(hc, axis=-1, keepdims=True)
        xc = hc - mu
        var = jnp.mean(xc * xc, axis=-1, keepdims=True)
        hn = xc * lax.rsqrt(var + LN_EPS) * lg_ref[...] + lb_ref[...]
        hs = _silu(hn)
        y = jnp.dot(hs.astype(BF16), wpw_ref[...], preferred_element_type=F32)
        o_ref[0, r0:r0 + rc, :] = (y * _silu(g_ref[0, r0:r0 + rc, :].astype(F32))).astype(o_ref.dtype)


def _conv_call(proj3, cw, cb, lg, lb, wpw, tc):
    nb, seq, _ = proj3.shape
    kern = functools.partial(_conv_kernel, tc=tc)
    const = lambda *shape: pl.BlockSpec(shape, lambda b, s: (0,) * len(shape))
    return pl.pallas_call(
        kern,
        grid=(nb, seq // tc),
        in_specs=[pl.BlockSpec((1, tc, GW), lambda b, s: (b, s, COL_VAL)),
                  pl.BlockSpec((1, tc, GW), lambda b, s: (b, s, COL_GLU)),
                  pl.BlockSpec((1, tc, GW), lambda b, s: (b, s, COL_GB)),
                  const(CONV_HALO, GW), const(1, GW), const(1, GW), const(1, GW), const(GW, GW)],
        out_specs=pl.BlockSpec((1, tc, GW), lambda b, s: (b, s, 0)),
        out_shape=jax.ShapeDtypeStruct((nb, seq, GW), BF16),
        scratch_shapes=[pltpu.VMEM((tc + CONV_HALO, GW), F32),
                        pltpu.VMEM((SUBLANES - 1, tc + CONV_HALO - SUBLANES, GW), F32)],
        compiler_params=pltpu.CompilerParams(
            dimension_semantics=("parallel", "arbitrary"), vmem_limit_bytes=VMEM_LIMIT),
        name="conv_group",
    )(proj3, proj3, proj3, cw, cb, lg, lb, wpw)


def _dsa_kernel(q_ref, g_ref, iq_ref, iw_ref, k_ref, v_ref, ika_ref, ikb_ref, bias_ref, o_ref,
                key_scr, half_scr, vt_scr, mask_scr, s0_scr, s1_scr, p_scr, m_scr, alpha_scr, acc_scr,
                *, tb, topk, seq):
    i = pl.program_id(1)
    nkb = i + 1
    nt = (((1,), (1,)), ((), ()))
    idx_bits = max(1, (seq - 1).bit_length())
    row_idx = lax.broadcasted_iota(I32, (tb, tb), 0)

    @pl.when(i == 0)
    def _():
        for j in range(seq // tb):
            vt_scr[j] = v_ref[0, j * tb:(j + 1) * tb, :].T

    iw_t = iw_ref[0].astype(F32).T[0:IDX_HEADS, :]

    kh_rows = tb // 2
    part_row = lax.broadcasted_iota(I32, (kh_rows, tb), 0)
    part_lane = lax.broadcasted_iota(I32, (kh_rows, tb), 1)

    def score_block(j, diagonal):
        for part in range(tb // kh_rows):
            off = pl.multiple_of(j * tb + part * kh_rows, kh_rows)
            kab = jnp.concatenate([ika_ref[0, pl.ds(off, kh_rows), :], ikb_ref[0, pl.ds(off, kh_rows), :]], axis=0)
            sc = jnp.zeros((kh_rows, tb), F32)
            for p in range(IDX_HEADS // 2):
                iqp = iq_ref[0, :, LANES * p:LANES * (p + 1)]
                d = lax.dot_general(kab, iqp, nt, preferred_element_type=F32)
                sc = (sc + iw_t[2 * p:2 * p + 1, :] * jnp.maximum(d[0:kh_rows], 0.0)
                      + iw_t[2 * p + 1:2 * p + 2, :] * jnp.maximum(d[kh_rows:2 * kh_rows], 0.0))
            if diagonal:
                sc = jnp.where(part * kh_rows + part_row <= part_lane, sc, -jnp.inf)
            bits = pltpu.bitcast(sc, I32)
            key = bits ^ ((bits >> 31) & 0x7FFFFFFF)
            key = jnp.where(key == -1, 0, key)
            key_scr[j, part * kh_rows:(part + 1) * kh_rows, :] = key
            half_scr[j, part * kh_rows:(part + 1) * kh_rows, :] = (key >> 16).astype(I16)

    def score_body(j, carry):
        score_block(j, diagonal=False)
        return carry

    lax.fori_loop(0, i, score_body, 0)
    score_block(i, diagonal=True)

    one16 = jnp.ones((), BF16)
    zero16 = jnp.zeros((), BF16)

    def count16(cand):
        c16 = jnp.broadcast_to(cand, (PACK16, tb)).astype(I16)

        def body(j, accs):
            a0, a1 = accs
            kt = half_scr[j]
            for r in range(0, tb // PACK16, 2):
                a0 = a0 + jnp.where(kt[PACK16 * r:PACK16 * (r + 1), :] >= c16, one16, zero16)
                a1 = a1 + jnp.where(kt[PACK16 * (r + 1):PACK16 * (r + 2), :] >= c16, one16, zero16)
            return a0, a1

        zeros = jnp.zeros((PACK16, tb), BF16)
        a0, a1 = lax.fori_loop(0, nkb, body, (zeros, zeros))
        return jnp.sum(a0.astype(F32) + a1.astype(F32), axis=0, keepdims=True)

    def search(target, nbits, offset):
        def body(it, state):
            best, above = state
            cand = best | lax.shift_left(jnp.int32(1), nbits - 1 - it)
            cnt = count16(cand - offset)
            ok = cnt >= target
            return jnp.where(ok, cand, best), jnp.where(ok, above, cnt)

        best, above = lax.fori_loop(0, nbits, body, (jnp.zeros((1, tb), I32), jnp.zeros((1, tb), F32)))
        return best - offset, above

    tau_hi, above = search(topk, 16, HALF_OFFSET)
    need = topk - above

    def low_body(j, carry):
        kt = key_scr[j]
        half_scr[j] = jnp.where((kt >> 16) == tau_hi, (kt & 0xFFFF) - HALF_OFFSET, -HALF_OFFSET).astype(I16)
        return carry

    lax.fori_loop(0, nkb, low_body, 0)
    tau_lo, above = search(need, 16, HALF_OFFSET)
    need = need - above
    tau = tau_hi * 65536 + (tau_lo + HALF_OFFSET)

    def tie_body(j, carry):
        half_scr[j] = jnp.where(key_scr[j] == tau, (seq - 1 - j * tb) - row_idx, -1).astype(I16)
        return carry

    lax.fori_loop(0, nkb, tie_body, 0)
    rank, _ = search(need, idx_bits, 0)
    few = tau <= KEY_NEG_INF
    tau = jnp.where(few, KEY_NEG_INF + 1, tau)
    last_tie = jnp.where(few, seq, seq - 1 - rank)

    def demote_body(j, carry):
        kt = key_scr[j]
        drop = (kt == tau) & (j * tb + row_idx > last_tie)
        key_scr[j] = jnp.where(drop, kt - 1, kt)
        return carry

    lax.fori_loop(0, nkb, demote_body, 0)

    m_scr[...] = jnp.full(m_scr.shape, NEG_BIG, F32)
    acc_scr[...] = jnp.zeros(acc_scr.shape, F32)
    c_log2 = HEAD_DIM ** -0.5 * LOG2_E
    ones_rows = jnp.ones((PACK16, tb), BF16)

    def logits_stage(j, s_scr):
        off = pl.multiple_of(j * tb, tb)
        mask_scr[...] = jnp.where(key_scr[j] >= tau, 0.0, NEG_BIG)
        bsel = jnp.minimum(i - j, 2)
        for h in range(ATT_HEADS):
            c0 = HEAD_DIM * h
            kh = k_ref[0, pl.ds(off, tb), c0:c0 + HEAD_DIM]
            qh = q_ref[0, :, c0:c0 + HEAD_DIM]
            s_scr[h] = (lax.dot_general(kh, qh, nt, preferred_element_type=F32) * c_log2
                        + bias_ref[bsel, h] + mask_scr[...])

    def softmax_stage(s_scr):
        m_new = []
        for h in range(ATT_HEADS):
            m_prev = m_scr[h]
            m_new.append(jnp.maximum(m_prev, jnp.max(s_scr[h], axis=0, keepdims=True)))
            alpha_scr[h] = jnp.exp2(m_prev - m_new[h])
            m_scr[h] = m_new[h]
        for h in range(ATT_HEADS):
            p_scr[h] = jnp.exp2(s_scr[h] - m_new[h]).astype(BF16)

    def value_stage(j):
        for h in range(ATT_HEADS):
            c0 = HEAD_DIM * h
            vt = jnp.concatenate([vt_scr[j, c0:c0 + HEAD_DIM, :], ones_rows], axis=0)
            acc_scr[h] = alpha_scr[h] * acc_scr[h] + jnp.dot(vt, p_scr[h], preferred_element_type=F32)

    def att_step(j, s_cur, s_next):
        value_stage(jnp.maximum(j - 1, 0))
        logits_stage(jnp.minimum(j + 1, nkb - 1), s_next)
        softmax_stage(s_cur)

    def att_body(j, carry):
        @pl.when(j % 2 == 0)
        def _():
            att_step(j, s0_scr, s1_scr)

        @pl.when(j % 2 == 1)
        def _():
            att_step(j, s1_scr, s0_scr)

        return carry

    p_scr[...] = jnp.zeros(p_scr.shape, BF16)
    alpha_scr[...] = jnp.ones(alpha_scr.shape, F32)
    logits_stage(0, s0_scr)
    lax.fori_loop(0, nkb, att_body, 0)
    value_stage(nkb - 1)

    y = jnp.concatenate(
        [(acc_scr[h, 0:HEAD_DIM, :] / acc_scr[h, HEAD_DIM:HEAD_DIM + 1, :]).T for h in range(ATT_HEADS)], axis=1)
    o_ref[0] = (y * _silu(g_ref[0].astype(F32))).astype(o_ref.dtype)


def _rel_bucket(dist):
    n = jnp.maximum(dist, 0)
    max_exact = REL_BUCKETS // 2
    nf = jnp.maximum(n, 1).astype(F32)
    large = max_exact + (jnp.log(nf / max_exact) / math.log(REL_MAX_DIST / max_exact)
                         * (REL_BUCKETS - max_exact)).astype(I32)
    large = jnp.minimum(large, REL_BUCKETS - 1)
    return jnp.where(n < max_exact, n, large)


def _bias_tiles(rel_bias, tb):
    assert tb >= REL_MAX_DIST
    w = 2 * tb - 1
    x = jnp.arange(w, dtype=I32) - (tb - 1)
    bucket = jnp.stack([_rel_bucket(delta + x) for delta in (0, tb, 2 * tb)])[:, None]
    table = rel_bias.astype(F32) * LOG2_E
    diag = jnp.zeros((3, ATT_HEADS, w), F32)
    for b in range(REL_BUCKETS):
        diag = jnp.where(bucket == b, table[b][None, :, None], diag)
    flat = jnp.tile(diag, (1, 1, tb + 1))
    return flat[:, :, tb - 1:tb - 1 + tb * (w - 1)].reshape(3, ATT_HEADS, tb, w - 1)[:, :, :, :tb]


def _dsa_call(proj3, bias, tb, topk):
    nb, seq, _ = proj3.shape
    nkb = seq // tb
    kern = functools.partial(_dsa_kernel, tb=tb, topk=float(topk), seq=seq)
    qblk = lambda col: pl.BlockSpec((1, tb, GW), lambda b, i: (b, i, col))
    return pl.pallas_call(
        kern,
        grid=(nb, nkb),
        in_specs=[qblk(COL_Q), qblk(COL_GC), qblk(COL_IQ),
                  pl.BlockSpec((1, tb, LANES), lambda b, i: (b, i, COL_IW)),
                  pl.BlockSpec((1, seq, GW), lambda b, i: (b, 0, COL_K)),
                  pl.BlockSpec((1, seq, GW), lambda b, i: (b, 0, COL_V)),
                  pl.BlockSpec((1, seq, LANES), lambda b, i: (b, 0, COL_IKA)),
                  pl.BlockSpec((1, seq, LANES), lambda b, i: (b, 0, COL_IKB)),
                  pl.BlockSpec(bias.shape, lambda b, i: (0, 0, 0, 0))],
        out_specs=pl.BlockSpec((1, tb, GW), lambda b, i: (b, i, 0)),
        out_shape=jax.ShapeDtypeStruct((nb, seq, GW), BF16),
        scratch_shapes=[pltpu.VMEM((nkb, tb, tb), I32),
                        pltpu.VMEM((nkb, tb, tb), I16),
                        pltpu.VMEM((nkb, GW, tb), BF16),
                        pltpu.VMEM((tb, tb), F32),
                        pltpu.VMEM((ATT_HEADS, tb, tb), F32),
                        pltpu.VMEM((ATT_HEADS, tb, tb), F32),
                        pltpu.VMEM((ATT_HEADS, tb, tb), BF16),
                        pltpu.VMEM((ATT_HEADS, 1, tb), F32),
                        pltpu.VMEM((ATT_HEADS, 1, tb), F32),
                        pltpu.VMEM((ATT_HEADS, HEAD_DIM + PACK16, tb), F32)],
        compiler_params=pltpu.CompilerParams(
            dimension_semantics=("arbitrary", "arbitrary"), vmem_limit_bytes=VMEM_LIMIT),
        name="dsa_group",
    )(proj3, proj3, proj3, proj3, proj3, proj3, proj3, proj3, bias)


def _mem_kernel(q_ref, g_ref, kv_ref, o_ref):
    nt = (((1,), (1,)), ((), ()))
    scale = HEAD_DIM ** -0.5
    outs = []
    for h in range(MEM_HEADS):
        c0 = HEAD_DIM * h
        qh = q_ref[0, :, c0:c0 + HEAD_DIM]
        kh = kv_ref[0, :, c0:c0 + HEAD_DIM]
        vh = kv_ref[0, :, GW + c0:GW + c0 + HEAD_DIM]
        s = lax.dot_general(qh, kh, nt, preferred_element_type=F32) * scale
        m = jnp.max(s, axis=1, keepdims=True)
        p = jnp.exp(s - m)
        l = jnp.sum(p, axis=1, keepdims=True)
        outs.append(jnp.dot(p.astype(BF16), vh, preferred_element_type=F32) / l)
    y = jnp.concatenate(outs, axis=1)
    o_ref[0] = (y * _silu(g_ref[0].astype(F32))).astype(o_ref.dtype)


def _mem_call(proj3, mkv, tm):
    nb, seq, _ = proj3.shape
    n_mem = mkv.shape[1]
    return pl.pallas_call(
        _mem_kernel,
        grid=(nb, seq // tm),
        in_specs=[pl.BlockSpec((1, tm, GW), lambda b, s: (b, s, COL_MQ)),
                  pl.BlockSpec((1, tm, GW), lambda b, s: (b, s, COL_GD)),
                  pl.BlockSpec((1, n_mem, 2 * GW), lambda b, s: (b, 0, 0))],
        out_specs=pl.BlockSpec((1, tm, GW), lambda b, s: (b, s, 0)),
        out_shape=jax.ShapeDtypeStruct((nb, seq, GW), BF16),
        compiler_params=pltpu.CompilerParams(
            dimension_semantics=("parallel", "parallel"), vmem_limit_bytes=VMEM_LIMIT),
        name="mem_group",
    )(proj3, proj3, mkv)


def _out_kernel(ya_ref, yb_ref, yc_ref, yd_ref, x_ref, w_ref, g_ref, b_ref, o_ref, *, alpha, tn):
    mixed = jnp.concatenate([ya_ref[...], yb_ref[...], yc_ref[...], yd_ref[...]], axis=1)
    for n0 in range(0, o_ref.shape[1], tn):
        o_ref[:, n0:n0 + tn] = (alpha * x_ref[:, n0:n0 + tn]
                                + jnp.dot(mixed, w_ref[:, n0:n0 + tn], preferred_element_type=F32))
    z = o_ref[...]
    mu = jnp.mean(z, axis=-1, keepdims=True)
    xc = z - mu
    var = jnp.mean(xc * xc, axis=-1, keepdims=True)
    o_ref[...] = xc * lax.rsqrt(var + LN_EPS) * g_ref[...] + b_ref[...]


def _out_call(ys, x2, w_out, ln_g, ln_b, alpha, tm):
    t = x2.shape[0]
    kern = functools.partial(_out_kernel, alpha=alpha, tn=GW)
    row = lambda width: pl.BlockSpec((tm, width), lambda r: (r, 0))
    const = lambda *shape: pl.BlockSpec(shape, lambda r: (0,) * len(shape))
    return pl.pallas_call(
        kern,
        grid=(t // tm,),
        in_specs=[row(GW), row(GW), row(GW), row(GW), row(D_MODEL),
                  const(4 * GW, D_MODEL), const(1, D_MODEL), const(1, D_MODEL)],
        out_specs=row(D_MODEL),
        out_shape=jax.ShapeDtypeStruct((t, D_MODEL), F32),
        compiler_params=pltpu.CompilerParams(
            dimension_semantics=("parallel",), vmem_limit_bytes=VMEM_LIMIT),
        name="out_deepnorm",
    )(*ys, x2, w_out, ln_g, ln_b)


def _pack_w_in(w):
    w = w.astype(BF16)
    d = w.shape[0]
    ik0 = 9 * GW + IDX_HEADS * IDX_DIM
    iw0 = ik0 + IDX_DIM
    md0 = iw0 + IDX_HEADS
    w_ik = w[:, ik0:iw0]
    zeros = lambda n: jnp.zeros((d, n), BF16)
    return jnp.concatenate(
        [w[:, :ik0], w[:, md0:md0 + 2 * GW],
         w_ik, zeros(LANES - IDX_DIM),
         zeros(LANES - IDX_DIM), w_ik,
         w[:, iw0:md0], zeros(LANES - IDX_HEADS),
         zeros(PACKED_WIDTH - 12 * GW - 3 * LANES)], axis=1)


def _pick(n, pref):
    while n % pref:
        pref //= 2
    return pref


def kernel(x, mem, w_in, ssm_lam_re, ssm_lam_im, ssm_log_dt, ssm_b_re, ssm_b_im, ssm_c_re, ssm_c_im,
           ssm_d, ssm_w_glu, conv_w, conv_b, conv_ln_g, conv_ln_b, conv_w_pw, rel_bias, mem_w_kv,
           w_out, ln_g, ln_b):
    nb, seq, d = x.shape
    depth = w_in.shape[0]
    t = nb * seq
    alpha = (2 * depth) ** 0.25
    topk = min(TOPK_MAX, seq // 4)
    tb = 256
    n_mem = mem.shape[1]

    bias = _bias_tiles(rel_bias, tb)
    mem2 = mem.reshape(nb * n_mem, d)
    x2 = x.reshape(t, d)
    for l in range(depth):
        proj = _in_proj(x2, _pack_w_in(w_in[l]), _pick(t, 512), GW)
        proj3 = proj.reshape(nb, seq, PACKED_WIDTH)
        mkv = _matmul(mem2, mem_w_kv[l].astype(BF16), _pick(nb * n_mem, 1024), GW, "mem_kv")
        mkv = mkv.reshape(nb, n_mem, 2 * GW)

        bt, cre, cim, abar = _ssm_params(ssm_lam_re[l], ssm_lam_im[l], ssm_log_dt[l], ssm_b_re[l],
                                         ssm_b_im[l], ssm_c_re[l], ssm_c_im[l])
        y_a = _ssm_call(proj3, bt, cre, cim, abar, ssm_d[l].reshape(1, GW).astype(F32),
                        ssm_w_glu[l].astype(BF16), _pick(seq, 128))
        cw = jnp.concatenate([conv_w[l].reshape(CONV_WIDTH, GW).astype(F32),
                              jnp.zeros((CONV_HALO - CONV_WIDTH, GW), F32)], axis=0)
        y_b = _conv_call(proj3, cw, conv_b[l].reshape(1, GW).astype(F32),
                         conv_ln_g[l].reshape(1, GW).astype(F32), conv_ln_b[l].reshape(1, GW).astype(F32),
                         conv_w_pw[l].astype(BF16), _pick(seq, 512))
        y_c = _dsa_call(proj3, bias, tb, topk)
        y_d = _mem_call(proj3, mkv, _pick(seq, 512))

        ys = [y.reshape(t, GW) for y in (y_a, y_b, y_c, y_d)]
        x2 = _out_call(ys, x2, w_out[l].astype(BF16), ln_g[l].reshape(1, d).astype(F32),
                       ln_b[l].reshape(1, d).astype(F32), alpha, _pick(t, 512))
    return x2.reshape(nb, seq, d)
```

```python
import functools
import math

import jax
import jax.numpy as jnp
import numpy as np
from jax import lax
from jax.experimental import pallas as pl
from jax.experimental.pallas import tpu as pltpu

F32 = jnp.float32
BF16 = jnp.bfloat16
I32 = jnp.int32
I16 = jnp.int16

D_MODEL = 2048
GW = 512
SSM_CH = 16
SSM_GROUPS = GW // SSM_CH
SSM_STATE = 64
SSM_COMPLEX = SSM_GROUPS * SSM_STATE
CONV_WIDTH = 31
CONV_HALO = 32
ATT_HEADS = 4
HEAD_DIM = GW // ATT_HEADS
IDX_HEADS = 8
IDX_DIM = 64
TOPK_MAX = 256
REL_BUCKETS = 32
REL_MAX_DIST = 128
MEM_HEADS = 4
LN_EPS = 1e-5
NEG_BIG = -1e30
LOG2_E = math.log2(math.e)

LANES = 128
SUBLANES = 8
PACK16 = 16
COL_U, COL_GA, COL_VAL, COL_GLU, COL_GB, COL_Q, COL_K, COL_V, COL_GC, COL_IQ, COL_MQ, COL_GD = range(12)
COL_IKA = 12 * (GW // LANES)
COL_IKB = COL_IKA + 1
COL_IW = COL_IKA + 2
PACKED_WIDTH = 13 * GW

KEY_NEG_INF = int(np.array(0xFF800000, np.uint32).astype(np.int64) ^ 0x7FFFFFFF) - (1 << 32)
HALF_OFFSET = 1 << 15

VMEM_LIMIT = 56 * 1024 * 1024


def _sigmoid(x):
    return 1.0 / (1.0 + jnp.exp(-x))


def _silu(x):
    return x * _sigmoid(x)


def _matmul_kernel(x_ref, w_ref, o_ref, xb_ref):
    @pl.when(pl.program_id(1) == 0)
    def _():
        xb_ref[...] = x_ref[...].astype(BF16)

    o_ref[...] = jnp.dot(xb_ref[...], w_ref[...], preferred_element_type=F32).astype(o_ref.dtype)


def _matmul(x, w, tm, tn, name):
    m, k = x.shape
    n = w.shape[1]
    return pl.pallas_call(
        _matmul_kernel,
        grid=(m // tm, n // tn),
        in_specs=[pl.BlockSpec((tm, k), lambda i, j: (i, 0)),
                  pl.BlockSpec((k, tn), lambda i, j: (0, j))],
        out_specs=pl.BlockSpec((tm, tn), lambda i, j: (i, j)),
        out_shape=jax.ShapeDtypeStruct((m, n), BF16),
        scratch_shapes=[pltpu.VMEM((tm, k), BF16)],
        compiler_params=pltpu.CompilerParams(
            dimension_semantics=("parallel", "arbitrary"), vmem_limit_bytes=VMEM_LIMIT),
        name=name,
    )(x, w)


def _in_proj_kernel(x_ref, w_ref, o_ref, *, tn):
    xb = x_ref[...].astype(BF16)
    for n0 in range(0, w_ref.shape[1], tn):
        o_ref[:, n0:n0 + tn] = jnp.dot(xb, w_ref[:, n0:n0 + tn], preferred_element_type=F32).astype(o_ref.dtype)


def _in_proj(x, w, tm, tn):
    m, k = x.shape
    n = w.shape[1]
    return pl.pallas_call(
        functools.partial(_in_proj_kernel, tn=tn),
        grid=(m // tm,),
        in_specs=[pl.BlockSpec((tm, k), lambda i: (i, 0)),
                  pl.BlockSpec((k, n), lambda i: (0, 0), pipeline_mode=pl.Buffered(1))],
        out_specs=pl.BlockSpec((tm, n), lambda i: (i, 0)),
        out_shape=jax.ShapeDtypeStruct((m, n), BF16),
        compiler_params=pltpu.CompilerParams(
            dimension_semantics=("parallel",), vmem_limit_bytes=VMEM_LIMIT),
        name="in_proj",
    )(x, w)


def _ssm_kernel(u_ref, g_ref, bt_ref, cre_ref, cim_ref, abar_ref, d_ref, wglu_ref, o_ref,
                st_ref, bu_ref, *, nb, lc):
    rows = nb * lc
    n_re = SSM_COMPLEX

    @pl.when(pl.program_id(0) == 0)
    def _():
        st_ref[...] = jnp.zeros_like(st_ref)

    n_ct = n_re // LANES
    u = pltpu.einshape("btc->tbc", u_ref[...]).reshape(rows, GW)
    for jt in range(2 * n_re // 256):
        k0 = LANES * ((jt % (n_re // 256)) // 2)
        res = jnp.dot(u[:, k0:k0 + LANES], bt_ref[jt], preferred_element_type=F32)
        bu_ref[2 * jt] = res[:, :LANES]
        bu_ref[2 * jt + 1] = res[:, LANES:]

    slab = 4
    for ch in range(n_ct // slab):
        tiles = tuple(range(slab * ch, slab * (ch + 1)))
        a_re = [jnp.broadcast_to(abar_ref[:, LANES * c:LANES * (c + 1)], (nb, LANES)) for c in tiles]
        a_im = [jnp.broadcast_to(abar_ref[:, n_re + LANES * c:n_re + LANES * (c + 1)], (nb, LANES))
                for c in tiles]

        def step(t, carry, tiles=tiles, a_re=a_re, a_im=a_im):
            r = pl.ds(pl.multiple_of(t * nb, nb), nb)
            new = []
            for n, c in enumerate(tiles):
                h_re, h_im = carry[n]
                n_hre = a_re[n] * h_re - a_im[n] * h_im + bu_ref[c, r, :]
                n_him = a_re[n] * h_im + a_im[n] * h_re + bu_ref[n_ct + c, r, :]
                bu_ref[c, r, :] = n_hre
                bu_ref[n_ct + c, r, :] = n_him
                new.append((n_hre, n_him))
            return tuple(new)

        init = tuple((st_ref[c], st_ref[n_ct + c]) for c in tiles)
        final = lax.fori_loop(0, lc, step, init, unroll=4)
        for n, c in enumerate(tiles):
            st_ref[c] = final[n][0]
            st_ref[n_ct + c] = final[n][1]

    ys = []
    for m in range(GW // LANES):
        s_re = jnp.concatenate([bu_ref[4 * m + n] for n in range(4)], axis=1).astype(BF16)
        s_im = jnp.concatenate([bu_ref[n_ct + 4 * m + n] for n in range(4)], axis=1).astype(BF16)
        ys.append(jnp.dot(s_re, cre_ref[m], preferred_element_type=F32)
                  + jnp.dot(s_im, cim_ref[m], preferred_element_type=F32))
    y = jnp.concatenate(ys, axis=1) + d_ref[...] * u.astype(F32)
    z = jnp.dot(y.astype(BF16), wglu_ref[...], preferred_element_type=F32)
    glu = pltpu.einshape("tbc->btc", (z[:, :GW] * _sigmoid(z[:, GW:])).reshape(lc, nb, GW))
    o_ref[...] = (glu * _silu(g_ref[...].astype(F32))).astype(o_ref.dtype)


def _ssm_params(lam_re, lam_im, log_dt, b_re, b_im, c_re, c_im):
    lr = lam_re.astype(F32)
    li = lam_im.astype(F32)
    dt = jnp.exp(log_dt.astype(F32))[:, None]
    mag = jnp.exp(lr * dt)
    abar_re = mag * jnp.cos(li * dt)
    abar_im = mag * jnp.sin(li * dt)
    den = lr * lr + li * li
    nr = abar_re - 1.0
    f_re = (nr * lr + abar_im * li) / den
    f_im = (abar_im * lr - nr * li) / den
    br = b_re.astype(F32)
    bi = b_im.astype(F32)
    bbar_re = f_re[..., None] * br - f_im[..., None] * bi
    bbar_im = f_re[..., None] * bi + f_im[..., None] * br
    n_tiles = SSM_COMPLEX // 256
    gpt = 256 // SSM_STATE
    gpr = LANES // SSM_CH
    meet = np.zeros((n_tiles, gpr, gpt), np.float32)
    for jj in range(n_tiles):
        for b in range(gpt):
            meet[jj, gpt * (jj % 2) + b, b] = 1.0
    bbar = jnp.stack([bbar_re, bbar_im]).reshape(2, n_tiles, gpt, SSM_STATE, SSM_CH)
    bt = jnp.einsum('pjbnc,jab->pjacbn', bbar, jnp.asarray(meet)).reshape(2 * n_tiles, LANES, 256).astype(BF16)
    eye = jnp.eye(gpr, dtype=F32)
    c_blk = lambda c: jnp.einsum('macn,ab->manbc', c.astype(F32).reshape(GW // LANES, gpr, SSM_CH, SSM_STATE),
                                 eye).reshape(GW // LANES, gpr * SSM_STATE, LANES).astype(BF16)
    cre = c_blk(c_re)
    cim = c_blk(-c_im)
    abar = jnp.concatenate([abar_re.reshape(1, SSM_COMPLEX), abar_im.reshape(1, SSM_COMPLEX)], axis=1)
    return bt, cre, cim, abar


def _ssm_call(proj3, bt, cre, cim, abar, d_skip, w_glu, lc):
    nb, seq, _ = proj3.shape
    kern = functools.partial(_ssm_kernel, nb=nb, lc=lc)
    const = lambda *shape: pl.BlockSpec(shape, lambda c: (0,) * len(shape))
    return pl.pallas_call(
        kern,
        grid=(seq // lc,),
        in_specs=[pl.BlockSpec((nb, lc, GW), lambda c: (0, c, COL_U)),
                  pl.BlockSpec((nb, lc, GW), lambda c: (0, c, COL_GA)),
                  const(*bt.shape), const(*cre.shape), const(*cim.shape), const(*abar.shape),
                  const(1, GW), const(GW, 2 * GW)],
        out_specs=pl.BlockSpec((nb, lc, GW), lambda c: (0, c, 0)),
        out_shape=jax.ShapeDtypeStruct((nb, seq, GW), BF16),
        scratch_shapes=[pltpu.VMEM((2 * SSM_COMPLEX // LANES, nb, LANES), F32),
                        pltpu.VMEM((2 * SSM_COMPLEX // LANES, nb * lc, LANES), F32)],
        compiler_params=pltpu.CompilerParams(
            dimension_semantics=("arbitrary",), vmem_limit_bytes=VMEM_LIMIT),
        name="ssm_group",
    )(proj3, proj3, bt, cre, cim, abar, d_skip, w_glu)


def _conv_kernel(val_ref, glu_ref, g_ref, cw_ref, cb_ref, lg_ref, lb_ref, wpw_ref, o_ref, h_ref, sh_ref, *, tc):
    @pl.when(pl.program_id(1) == 0)
    def _():
        h_ref[0:CONV_HALO, :] = jnp.zeros((CONV_HALO, GW), F32)

    @pl.when(pl.program_id(1) != 0)
    def _():
        h_ref[0:CONV_HALO, :] = h_ref[tc:tc + CONV_HALO, :]

    val = val_ref[0].astype(F32)
    h_ref[CONV_HALO:CONV_HALO + tc, :] = val * _sigmoid(glu_ref[0].astype(F32))

    span = tc + CONV_HALO - SUBLANES
    for r in range(1, SUBLANES):
        sh_ref[r - 1] = h_ref[r:r + span, :]

    rc = 64
    first = CONV_HALO - (CONV_WIDTH - 1)
    for r0 in range(0, tc, rc):
        acc = jnp.zeros((rc, GW), F32)
        for k in range(CONV_WIDTH):
            a, r = divmod(first + k, SUBLANES)
            lo = r0 + SUBLANES * a
            src = h_ref[lo:lo + rc, :] if r == 0 else sh_ref[r - 1, lo:lo + rc, :]
            acc = acc + cw_ref[k:k + 1, :] * src
        hc = acc + cb_ref[...]
        mu = jnp.mean(hc, axis=-1, keepdims=True)
        xc = hc - mu
        var = jnp.mean(xc * xc, axis=-1, keepdims=True)
        hn = xc * lax.rsqrt(var + LN_EPS) * lg_ref[...] + lb_ref[...]
        hs = _silu(hn)
        y = jnp.dot(hs.astype(BF16), wpw_ref[...], preferred_element_type=F32)
        o_ref[0, r0:r0 + rc, :] = (y * _silu(g_ref[0, r0:r0 + rc, :].astype(F32))).astype(o_ref.dtype)


def _conv_call(proj3, cw, cb, lg, lb, wpw, tc):
    nb, seq, _ = proj3.shape
    kern = functools.partial(_conv_kernel, tc=tc)
    const = lambda *shape: pl.BlockSpec(shape, lambda b, s: (0,) * len(shape))
    return pl.pallas_call(
        kern,
        grid=(nb, seq // tc),
        in_specs=[pl.BlockSpec((1, tc, GW), lambda b, s: (b, s, COL_VAL)),
                  pl.BlockSpec((1, tc, GW), lambda b, s: (b, s, COL_GLU)),
                  pl.BlockSpec((1, tc, GW), lambda b, s: (b, s, COL_GB)),
                  const(CONV_HALO, GW), const(1, GW), const(1, GW), const(1, GW), const(GW, GW)],
        out_specs=pl.BlockSpec((1, tc, GW), lambda b, s: (b, s, 0)),
        out_shape=jax.ShapeDtypeStruct((nb, seq, GW), BF16),
        scratch_shapes=[pltpu.VMEM((tc + CONV_HALO, GW), F32),
                        pltpu.VMEM((SUBLANES - 1, tc + CONV_HALO - SUBLANES, GW), F32)],
        compiler_params=pltpu.CompilerParams(
            dimension_semantics=("parallel", "arbitrary"), vmem_limit_bytes=VMEM_LIMIT),
        name="conv_group",
    )(proj3, proj3, proj3, cw, cb, lg, lb, wpw)


def _dsa_kernel(q_ref, g_ref, iq_ref, iw_ref, k_ref, v_ref, ika_ref, ikb_ref, bias_ref, o_ref,
                key_scr, half_scr, vt_scr, mask_scr, s0_scr, s1_scr, p_scr, m_scr, alpha_scr, acc_scr,
                *, tb, topk, seq):
    i = pl.program_id(1)
    nkb = i + 1
    nt = (((1,), (1,)), ((), ()))
    idx_bits = max(1, (seq - 1).bit_length())
    row_idx = lax.broadcasted_iota(I32, (tb, tb), 0)

    @pl.when(i == 0)
    def _():
        for j in range(seq // tb):
            vt_scr[j] = v_ref[0, j * tb:(j + 1) * tb, :].T

    iw_t = iw_ref[0].astype(F32).T[0:IDX_HEADS, :]

    kh_rows = tb // 2
    part_row = lax.broadcasted_iota(I32, (kh_rows, tb), 0)
    part_lane = lax.broadcasted_iota(I32, (kh_rows, tb), 1)

    def score_block(j, diagonal):
        for part in range(tb // kh_rows):
            off = pl.multiple_of(j * tb + part * kh_rows, kh_rows)
            kab = jnp.concatenate([ika_ref[0, pl.ds(off, kh_rows), :], ikb_ref[0, pl.ds(off, kh_rows), :]], axis=0)
            sc = jnp.zeros((kh_rows, tb), F32)
            for p in range(IDX_HEADS // 2):
                iqp = iq_ref[0, :, LANES * p:LANES * (p + 1)]
                d = lax.dot_general(kab, iqp, nt, preferred_element_type=F32)
                sc = (sc + iw_t[2 * p:2 * p + 1, :] * jnp.maximum(d[0:kh_rows], 0.0)
                      + iw_t[2 * p + 1:2 * p + 2, :] * jnp.maximum(d[kh_rows:2 * kh_rows], 0.0))
            if diagonal:
                sc = jnp.where(part * kh_rows + part_row <= part_lane, sc, -jnp.inf)
            bits = pltpu.bitcast(sc, I32)
            key = bits ^ ((bits >> 31) & 0x7FFFFFFF)
            key = jnp.where(key == -1, 0, key)
            key_scr[j, part * kh_rows:(part + 1) * kh_rows, :] = key
            half_scr[j, part * kh_rows:(part + 1) * kh_rows, :] = (key >> 16).astype(I16)

    def score_body(jj, carry):
        score_block(2 * jj, diagonal=False)
        score_block(2 * jj + 1, diagonal=False)
        return carry

    lax.fori_loop(0, i // 2, score_body, 0)

    @pl.when(i % 2 == 1)
    def _():
        score_block(i - 1, diagonal=False)

    score_block(i, diagonal=True)

    one16 = jnp.ones((), BF16)
    zero16 = jnp.zeros((), BF16)

    def count16(cand):
        c16 = jnp.broadcast_to(cand, (PACK16, tb)).astype(I16)

        def body(j, accs):
            a0, a1 = accs
            kt = half_scr[j]
            for r in range(0, tb // PACK16, 2):
                a0 = a0 + jnp.where(kt[PACK16 * r:PACK16 * (r + 1), :] >= c16, one16, zero16)
                a1 = a1 + jnp.where(kt[PACK16 * (r + 1):PACK16 * (r + 2), :] >= c16, one16, zero16)
            return a0, a1

        zeros = jnp.zeros((PACK16, tb), BF16)
        a0, a1 = lax.fori_loop(0, nkb, body, (zeros, zeros))
        return jnp.sum(a0.astype(F32) + a1.astype(F32), axis=0, keepdims=True)

    def search(target, nbits, offset):
        def body(it, state):
            best, above, reached = state
            cand = best | lax.shift_left(jnp.int32(1), nbits - 1 - it)
            cnt = count16(cand - offset)
            ok = cnt >= target
            return jnp.where(ok, cand, best), jnp.where(ok, above, cnt), jnp.where(ok, cnt, reached)

        everything = jnp.full((1, tb), 1.0, F32) * (nkb * tb).astype(F32)
        best, above, reached = lax.fori_loop(
            0, nbits, body, (jnp.zeros((1, tb), I32), jnp.zeros((1, tb), F32), everything))
        return best - offset, above, reached

    tau_hi, above, _ = search(topk, 16, HALF_OFFSET)
    need = topk - above

    def low_body(j, carry):
        kt = key_scr[j]
        half_scr[j] = jnp.where((kt >> 16) == tau_hi, (kt & 0xFFFF) - HALF_OFFSET, -HALF_OFFSET).astype(I16)
        return carry

    lax.fori_loop(0, nkb, low_body, 0)
    tau_lo, above, reached = search(need, 16, HALF_OFFSET)
    tau_tie = tau_hi * 65536 + (tau_lo + HALF_OFFSET)
    few = tau_tie <= KEY_NEG_INF
    tau = jnp.where(few, KEY_NEG_INF + 1, tau_tie)
    tied = jnp.where(few, 0.0, jnp.where(reached > need, 1.0, 0.0))
    need = need - above

    @pl.when(jnp.max(tied) > 0.0)
    def _():
        def tie_body(j, carry):
            half_scr[j] = jnp.where(key_scr[j] == tau_tie, (seq - 1 - j * tb) - row_idx, -1).astype(I16)
            return carry

        lax.fori_loop(0, nkb, tie_body, 0)
        rank, _, _ = search(need, idx_bits, 0)
        last_tie = jnp.where(few, seq, seq - 1 - rank)

        def demote_body(j, carry):
            kt = key_scr[j]
            drop = (kt == tau) & (j * tb + row_idx > last_tie)
            key_scr[j] = jnp.where(drop, kt - 1, kt)
            return carry

        lax.fori_loop(0, nkb, demote_body, 0)

    m_scr[...] = jnp.full(m_scr.shape, NEG_BIG, F32)
    acc_scr[...] = jnp.zeros(acc_scr.shape, F32)
    c_log2 = HEAD_DIM ** -0.5 * LOG2_E
    ones_rows = jnp.ones((PACK16, tb), BF16)

    def logits_stage(j, s_scr):
        off = pl.multiple_of(j * tb, tb)
        mask_scr[...] = jnp.where(key_scr[j] >= tau, 0.0, NEG_BIG)
        bsel = jnp.minimum(i - j, 2)
        for h in range(ATT_HEADS):
            c0 = HEAD_DIM * h
            kh = k_ref[0, pl.ds(off, tb), c0:c0 + HEAD_DIM]
            qh = q_ref[0, :, c0:c0 + HEAD_DIM]
            s_scr[h] = (lax.dot_general(kh, qh, nt, preferred_element_type=F32) * c_log2
                        + bias_ref[bsel, h] + mask_scr[...])

    def softmax_stage(s_scr):
        m_new = []
        for h in range(ATT_HEADS):
            m_prev = m_scr[h]
            m_new.append(jnp.maximum(m_prev, jnp.max(s_scr[h], axis=0, keepdims=True)))
            alpha_scr[h] = jnp.exp2(m_prev - m_new[h])
            m_scr[h] = m_new[h]
        for h in range(ATT_HEADS):
            p_scr[h] = jnp.exp2(s_scr[h] - m_new[h]).astype(BF16)

    def value_stage(j):
        for h in range(ATT_HEADS):
            c0 = HEAD_DIM * h
            vt = jnp.concatenate([vt_scr[j, c0:c0 + HEAD_DIM, :], ones_rows], axis=0)
            acc_scr[h] = alpha_scr[h] * acc_scr[h] + jnp.dot(vt, p_scr[h], preferred_element_type=F32)

    def att_step(j, s_cur, s_next):
        value_stage(jnp.maximum(j - 1, 0))
        logits_stage(jnp.minimum(j + 1, nkb - 1), s_next)
        softmax_stage(s_cur)

    def att_body(j, carry):
        @pl.when(j % 2 == 0)
        def _():
            att_step(j, s0_scr, s1_scr)

        @pl.when(j % 2 == 1)
        def _():
            att_step(j, s1_scr, s0_scr)

        return carry

    p_scr[...] = jnp.zeros(p_scr.shape, BF16)
    alpha_scr[...] = jnp.ones(alpha_scr.shape, F32)
    logits_stage(0, s0_scr)
    lax.fori_loop(0, nkb, att_body, 0)
    value_stage(nkb - 1)

    y = jnp.concatenate(
        [(acc_scr[h, 0:HEAD_DIM, :] / acc_scr[h, HEAD_DIM:HEAD_DIM + 1, :]).T for h in range(ATT_HEADS)], axis=1)
    o_ref[0] = (y * _silu(g_ref[0].astype(F32))).astype(o_ref.dtype)


def _rel_bucket(dist):
    n = jnp.maximum(dist, 0)
    max_exact = REL_BUCKETS // 2
    nf = jnp.maximum(n, 1).astype(F32)
    large = max_exact + (jnp.log(nf / max_exact) / math.log(REL_MAX_DIST / max_exact)
                         * (REL_BUCKETS - max_exact)).astype(I32)
    large = jnp.minimum(large, REL_BUCKETS - 1)
    return jnp.where(n < max_exact, n, large)


def _bias_tiles(rel_bias, tb):
    assert tb >= REL_MAX_DIST
    s = jnp.arange(tb, dtype=I32)[:, None]
    q = jnp.arange(tb, dtype=I32)[None, :]
    bucket = jnp.stack([_rel_bucket(delta + q - s) for delta in (0, tb, 2 * tb)])[:, None]
    table = rel_bias.astype(F32) * LOG2_E
    tiles = jnp.zeros((3, ATT_HEADS, tb, tb), F32)
    for b in range(REL_BUCKETS):
        tiles = jnp.where(bucket == b, table[b][None, :, None, None], tiles)
    return tiles


def _dsa_call(proj3, bias, tb, topk):
    nb, seq, _ = proj3.shape
    nkb = seq // tb
    kern = functools.partial(_dsa_kernel, tb=tb, topk=float(topk), seq=seq)
    qblk = lambda col: pl.BlockSpec((1, tb, GW), lambda b, i: (b, i, col))
    return pl.pallas_call(
        kern,
        grid=(nb, nkb),
        in_specs=[qblk(COL_Q), qblk(COL_GC), qblk(COL_IQ),
                  pl.BlockSpec((1, tb, LANES), lambda b, i: (b, i, COL_IW)),
                  pl.BlockSpec((1, seq, GW), lambda b, i: (b, 0, COL_K)),
                  pl.BlockSpec((1, seq, GW), lambda b, i: (b, 0, COL_V)),
                  pl.BlockSpec((1, seq, LANES), lambda b, i: (b, 0, COL_IKA)),
                  pl.BlockSpec((1, seq, LANES), lambda b, i: (b, 0, COL_IKB)),
                  pl.BlockSpec(bias.shape, lambda b, i: (0, 0, 0, 0))],
        out_specs=pl.BlockSpec((1, tb, GW), lambda b, i: (b, i, 0)),
        out_shape=jax.ShapeDtypeStruct((nb, seq, GW), BF16),
        scratch_shapes=[pltpu.VMEM((nkb, tb, tb), I32),
                        pltpu.VMEM((nkb, tb, tb), I16),
                        pltpu.VMEM((nkb, GW, tb), BF16),
                        pltpu.VMEM((tb, tb), F32),
                        pltpu.VMEM((ATT_HEADS, tb, tb), F32),
                        pltpu.VMEM((ATT_HEADS, tb, tb), F32),
                        pltpu.VMEM((ATT_HEADS, tb, tb), BF16),
                        pltpu.VMEM((ATT_HEADS, 1, tb), F32),
                        pltpu.VMEM((ATT_HEADS, 1, tb), F32),
                        pltpu.VMEM((ATT_HEADS, HEAD_DIM + PACK16, tb), F32)],
        compiler_params=pltpu.CompilerParams(
            dimension_semantics=("arbitrary", "arbitrary"), vmem_limit_bytes=VMEM_LIMIT),
        name="dsa_group",
    )(proj3, proj3, proj3, proj3, proj3, proj3, proj3, proj3, bias)


def _mem_kernel(q_ref, g_ref, kv_ref, o_ref):
    nt = (((1,), (1,)), ((), ()))
    scale = HEAD_DIM ** -0.5
    outs = []
    for h in range(MEM_HEADS):
        c0 = HEAD_DIM * h
        qh = q_ref[0, :, c0:c0 + HEAD_DIM]
        kh = kv_ref[0, :, c0:c0 + HEAD_DIM]
        vh = kv_ref[0, :, GW + c0:GW + c0 + HEAD_DIM]
        s = lax.dot_general(qh, kh, nt, preferred_element_type=F32) * scale
        m = jnp.max(s, axis=1, keepdims=True)
        p = jnp.exp(s - m)
        l = jnp.sum(p, axis=1, keepdims=True)
        outs.append(jnp.dot(p.astype(BF16), vh, preferred_element_type=F32) / l)
    y = jnp.concatenate(outs, axis=1)
    o_ref[0] = (y * _silu(g_ref[0].astype(F32))).astype(o_ref.dtype)


def _mem_call(proj3, mkv, tm):
    nb, seq, _ = proj3.shape
    n_mem = mkv.shape[1]
    return pl.pallas_call(
        _mem_kernel,
        grid=(nb, seq // tm),
        in_specs=[pl.BlockSpec((1, tm, GW), lambda b, s: (b, s, COL_MQ)),
                  pl.BlockSpec((1, tm, GW), lambda b, s: (b, s, COL_GD)),
                  pl.BlockSpec((1, n_mem, 2 * GW), lambda b, s: (b, 0, 0))],
        out_specs=pl.BlockSpec((1, tm, GW), lambda b, s: (b, s, 0)),
        out_shape=jax.ShapeDtypeStruct((nb, seq, GW), BF16),
        compiler_params=pltpu.CompilerParams(
            dimension_semantics=("parallel", "parallel"), vmem_limit_bytes=VMEM_LIMIT),
        name="mem_group",
    )(proj3, proj3, mkv)


def _out_kernel(ya_ref, yb_ref, yc_ref, yd_ref, x_ref, w_ref, g_ref, b_ref, o_ref, *, alpha, tn):
    mixed = jnp.concatenate([ya_ref[...], yb_ref[...], yc_ref[...], yd_ref[...]], axis=1)
    for n0 in range(0, o_ref.shape[1], tn):
        o_ref[:, n0:n0 + tn] = (alpha * x_ref[:, n0:n0 + tn]
                                + jnp.dot(mixed, w_ref[:, n0:n0 + tn], preferred_element_type=F32))
    z = o_ref[...]
    mu = jnp.mean(z, axis=-1, keepdims=True)
    xc = z - mu
    var = jnp.mean(xc * xc, axis=-1, keepdims=True)
    o_ref[...] = xc * lax.rsqrt(var + LN_EPS) * g_ref[...] + b_ref[...]


def _out_call(ys, x2, w_out, ln_g, ln_b, alpha, tm):
    t = x2.shape[0]
    kern = functools.partial(_out_kernel, alpha=alpha, tn=GW)
    row = lambda width: pl.BlockSpec((tm, width), lambda r: (r, 0))
    const = lambda *shape: pl.BlockSpec(shape, lambda r: (0,) * len(shape))
    return pl.pallas_call(
        kern,
        grid=(t // tm,),
        in_specs=[row(GW), row(GW), row(GW), row(GW), row(D_MODEL),
                  const(4 * GW, D_MODEL), const(1, D_MODEL), const(1, D_MODEL)],
        out_specs=row(D_MODEL),
        out_shape=jax.ShapeDtypeStruct((t, D_MODEL), F32),
        compiler_params=pltpu.CompilerParams(
            dimension_semantics=("parallel",), vmem_limit_bytes=VMEM_LIMIT),
        name="out_deepnorm",
    )(*ys, x2, w_out, ln_g, ln_b)


def _pack_w_in(w):
    w = w.astype(BF16)
    d = w.shape[0]
    ik0 = 9 * GW + IDX_HEADS * IDX_DIM
    iw0 = ik0 + IDX_DIM
    md0 = iw0 + IDX_HEADS
    w_ik = w[:, ik0:iw0]
    zeros = lambda n: jnp.zeros((d, n), BF16)
    return jnp.concatenate(
        [w[:, :ik0], w[:, md0:md0 + 2 * GW],
         w_ik, zeros(LANES - IDX_DIM),
         zeros(LANES - IDX_DIM), w_ik,
         w[:, iw0:md0], zeros(LANES - IDX_HEADS),
         zeros(PACKED_WIDTH - 12 * GW - 3 * LANES)], axis=1)


def _pick(n, pref):
    while n % pref:
        pref //= 2
    return pref


def kernel(x, mem, w_in, ssm_lam_re, ssm_lam_im, ssm_log_dt, ssm_b_re, ssm_b_im, ssm_c_re, ssm_c_im,
           ssm_d, ssm_w_glu, conv_w, conv_b, conv_ln_g, conv_ln_b, conv_w_pw, rel_bias, mem_w_kv,
           w_out, ln_g, ln_b):
    nb, seq, d = x.shape
    depth = w_in.shape[0]
    t = nb * seq
    alpha = (2 * depth) ** 0.25
    topk = min(TOPK_MAX, seq // 4)
    tb = 256
    n_mem = mem.shape[1]

    bias = _bias_tiles(rel_bias, tb)
    mem2 = mem.reshape(nb * n_mem, d)
    x2 = x.reshape(t, d)
    for l in range(depth):
        proj = _in_proj(x2, _pack_w_in(w_in[l]), _pick(t, 512), GW)
        proj3 = proj.reshape(nb, seq, PACKED_WIDTH)
        mkv = _matmul(mem2, mem_w_kv[l].astype(BF16), _pick(nb * n_mem, 1024), GW, "mem_kv")
        mkv = mkv.reshape(nb, n_mem, 2 * GW)

        bt, cre, cim, abar = _ssm_params(ssm_lam_re[l], ssm_lam_im[l], ssm_log_dt[l], ssm_b_re[l],
                                         ssm_b_im[l], ssm_c_re[l], ssm_c_im[l])
        y_a = _ssm_call(proj3, bt, cre, cim, abar, ssm_d[l].reshape(1, GW).astype(F32),
                        ssm_w_glu[l].astype(BF16), _pick(seq, 128))
        cw = jnp.concatenate([conv_w[l].reshape(CONV_WIDTH, GW).astype(F32),
                              jnp.zeros((CONV_HALO - CONV_WIDTH, GW), F32)], axis=0)
        y_b = _conv_call(proj3, cw, conv_b[l].reshape(1, GW).astype(F32),
                         conv_ln_g[l].reshape(1, GW).astype(F32), conv_ln_b[l].reshape(1, GW).astype(F32),
                         conv_w_pw[l].astype(BF16), _pick(seq, 512))
        y_c = _dsa_call(proj3, bias, tb, topk)
        y_d = _mem_call(proj3, mkv, _pick(seq, 512))

        ys = [y.reshape(t, GW) for y in (y_a, y_b, y_c, y_d)]
        x2 = _out_call(ys, x2, w_out[l].astype(BF16), ln_g[l].reshape(1, d).astype(F32),
                       ln_b[l].reshape(1, d).astype(F32), alpha, _pick(t, 512))
    return x2.reshape(nb, seq, d)
```
